```python
import math
import jax, jax.numpy as jnp
from jax import lax
import numpy as np

D_MODEL = 1024
BATCH = 4
SEQ = 8192
DEPTH = 2
DEC_BATCH = 16
DEC_SEQ = 2048
PAST_LEN = 128

GRID_W = 64
HEAD_DIM = 64
D_FF = 2816
N_BRANCH = 4
BRANCH_W = 512
CONV_W = BRANCH_W
POOL_W = BRANCH_W
POOL_WINDOWS = (2, 4, 8, 16)
POOL_GROUP = POOL_W // 4
GQA_HEADS = 8
GQA_KV = 2
GQA_GROUP = GQA_HEADS // GQA_KV
DIFF_HEADS = 4
DIFF_V = 2 * HEAD_DIM
AXIAL_THETA = 10000.0
ROPE_THETA = 500000.0
ROPE_DIM = HEAD_DIM // 4
Q_BLOCK = 128
EPS = 1e-6
IN_SIZES = (CONV_W, CONV_W, CONV_W, POOL_W,
            GQA_HEADS * HEAD_DIM, GQA_KV * HEAD_DIM, GQA_KV * HEAD_DIM,
            DIFF_HEADS * 2 * HEAD_DIM, DIFF_HEADS * 2 * HEAD_DIM, DIFF_HEADS * DIFF_V,
            N_BRANCH * D_MODEL)
IN_W = sum(IN_SIZES)

kernel_name = "hybrid_gated_conv_pool_gqa_diff_encoder"


def rms_norm(x, g):
    xf = x.astype(jnp.float32)
    y = xf * lax.rsqrt(jnp.mean(xf * xf, axis=-1, keepdims=True) + EPS)
    return (y * g.astype(jnp.float32)).astype(x.dtype)


def swiglu(x, g, w_in, w_out):
    h = rms_norm(x, g) @ w_in
    a, b = jnp.split(h, 2, axis=-1)
    return (jax.nn.silu(a) * b) @ w_out


def rope(x, pos, theta):
    n = x.shape[-1]
    half = n // 2
    freqs = jnp.exp(-math.log(theta) * jnp.arange(half, dtype=jnp.float32) * (2.0 / n))
    ang = pos.astype(jnp.float32)[:, None] * freqs[None, :]
    shp = (ang.shape[0],) + (1,) * (x.ndim - 3) + (half,)
    cos = jnp.cos(ang).reshape(shp)
    sin = jnp.sin(ang).reshape(shp)
    xf = x.astype(jnp.float32)
    x1, x2 = xf[..., :half], xf[..., half:]
    return jnp.concatenate([x1 * cos - x2 * sin, x2 * cos + x1 * sin], axis=-1).astype(x.dtype)


def axial_rope(x, row, col):
    half = HEAD_DIM // 2
    return jnp.concatenate([rope(x[..., :half], row, AXIAL_THETA),
                            rope(x[..., half:], col, AXIAL_THETA)], axis=-1)


def partial_rope(x, pos):
    return jnp.concatenate([rope(x[..., :ROPE_DIM], pos, ROPE_THETA), x[..., ROPE_DIM:]], axis=-1)


def short_conv(xin, b_gate, c_gate, w):
    z = c_gate * xin
    zp = jnp.pad(z, ((0, 0), (1, 1), (0, 0)))
    y = w[0] * zp[:, :-2] + w[1] * zp[:, 1:-1] + w[2] * zp[:, 2:]
    return b_gate * y


def pool_mixer(p, w, scale):
    B, S, _ = p.shape
    t = jnp.arange(S)
    pf = p.astype(jnp.float32)
    cs = jnp.concatenate([jnp.zeros((B, 1, POOL_W), jnp.float32), jnp.cumsum(pf, axis=1)], axis=1)
    outs = []
    for gi, win in enumerate(POOL_WINDOWS):
        lo = jnp.clip(t - win // 2, 0, S)
        hi = jnp.clip(t - win // 2 + win, 0, S)
        c = cs[..., gi * POOL_GROUP:(gi + 1) * POOL_GROUP]
        s = jnp.take(c, hi, axis=1) - jnp.take(c, lo, axis=1)
        cnt = (hi - lo).astype(jnp.float32)[:, None]
        outs.append(s / cnt - pf[..., gi * POOL_GROUP:(gi + 1) * POOL_GROUP])
    m = jnp.stack(outs, axis=2).astype(p.dtype)
    y = jnp.einsum('bsgc,gcd->bsgd', m, w).reshape(B, S, POOL_W)
    return y * scale


def gqa_attention(q, k, v, q_norm, k_norm, row, col):
    B, S, _ = q.shape
    q = axial_rope(rms_norm(q.reshape(B, S, GQA_HEADS, HEAD_DIM), q_norm), row, col)
    k = axial_rope(rms_norm(k.reshape(B, S, GQA_KV, HEAD_DIM), k_norm), row, col)
    v = v.reshape(B, S, GQA_KV, HEAD_DIM)
    nb = S // Q_BLOCK
    qb = q.reshape(B, nb, Q_BLOCK, GQA_KV, GQA_GROUP, HEAD_DIM).transpose(1, 0, 2, 3, 4, 5)
    scale = HEAD_DIM ** -0.5

    def block(qi):
        s = jnp.einsum('bqkgd,bskd->bkgqs', qi, k).astype(jnp.float32) * scale
        pr = jax.nn.softmax(s, axis=-1).astype(v.dtype)
        return jnp.einsum('bkgqs,bskd->bqkgd', pr, v)

    o = lax.map(block, qb)
    return o.transpose(1, 0, 2, 3, 4, 5).reshape(B, S, GQA_HEADS * HEAD_DIM)


def diff_attention(q, k, v, q_norm, k_norm, lam_vec, out_norm, lambda_init, pos):
    B, S, _ = q.shape
    q = partial_rope(rms_norm(q.reshape(B, S, DIFF_HEADS, 2, HEAD_DIM), q_norm), pos)
    k = partial_rope(rms_norm(k.reshape(B, S, DIFF_HEADS, 2, HEAD_DIM), k_norm), pos)
    v = v.reshape(B, S, DIFF_HEADS, DIFF_V)
    lv = lam_vec.astype(jnp.float32)
    lam = jnp.exp(jnp.sum(lv[0] * lv[1])) - jnp.exp(jnp.sum(lv[2] * lv[3])) + lambda_init
    nb = S // Q_BLOCK
    qb = q.reshape(B, nb, Q_BLOCK, DIFF_HEADS, 2, HEAD_DIM).transpose(1, 0, 2, 3, 4, 5)
    scale = HEAD_DIM ** -0.5

    def block(qi):
        s = jnp.einsum('bqhcd,bshcd->bhcqs', qi, k).astype(jnp.float32) * scale
        pr = jax.nn.softmax(s, axis=-1)
        a = (pr[:, :, 0] - lam * pr[:, :, 1]).astype(v.dtype)
        return jnp.einsum('bhqs,bshe->bqhe', a, v)

    o = lax.map(block, qb)
    o = o.transpose(1, 0, 2, 3, 4).reshape(B, S, DIFF_HEADS, DIFF_V)
    o = rms_norm(o, out_norm) * (1.0 - lambda_init)
    return o.reshape(B, S, DIFF_HEADS * DIFF_V)


def mixer(h, layer_idx, mix_norm, w_in, b_gate, conv_w, pool_w, pool_scale,
          attn_q_norm, attn_k_norm, diff_q_norm, diff_k_norm, diff_lambda, diff_out_norm,
          w_branch, w_out, row, col, pos):
    B, S, _ = h.shape
    u = rms_norm(h, mix_norm)
    z = u @ w_in
    offs = np.cumsum(IN_SIZES)[:-1].tolist()
    a_x, a_b, a_c, p_in, cq, ck, cv, dq, dk, dv, g = jnp.split(z, offs, axis=-1)
    lambda_init = 0.8 - 0.6 * math.exp(-0.3 * layer_idx)
    branches = (
        short_conv(a_x, a_b, a_c, conv_w),
        pool_mixer(p_in, pool_w, pool_scale),
        gqa_attention(cq, ck, cv, attn_q_norm, attn_k_norm, row, col),
        diff_attention(dq, dk, dv, diff_q_norm, diff_k_norm, diff_lambda, diff_out_norm, lambda_init, pos),
    )
    gates = jax.nn.sigmoid(g + b_gate).reshape(B, S, N_BRANCH, D_MODEL)
    merged = gates[:, :, 0] * (branches[0] @ w_branch[0])
    for n in range(1, N_BRANCH):
        merged = merged + gates[:, :, n] * (branches[n] @ w_branch[n])
    return merged @ w_out


def trunk(x, ffn1_norm, ffn1_w_in, ffn1_w_out, mix_norm, w_in, b_gate, conv_w, pool_w, pool_scale,
          attn_q_norm, attn_k_norm, diff_q_norm, diff_k_norm, diff_lambda, diff_out_norm,
          w_branch, w_out, ffn2_norm, ffn2_w_in, ffn2_w_out):
    S = x.shape[1]
    rows = S // GRID_W
    row = jnp.repeat(jnp.arange(rows), GRID_W)
    col = jnp.tile(jnp.arange(GRID_W), rows)
    pos = jnp.arange(S)
    for i in range(DEPTH):
        x = x + 0.5 * swiglu(x, ffn1_norm[i], ffn1_w_in[i], ffn1_w_out[i])
        x = x + mixer(x, i, mix_norm[i], w_in[i], b_gate[i], conv_w[i], pool_w[i], pool_scale[i],
                      attn_q_norm[i], attn_k_norm[i], diff_q_norm[i], diff_k_norm[i],
                      diff_lambda[i], diff_out_norm[i], w_branch[i], w_out[i], row, col, pos)
        x = x + 0.5 * swiglu(x, ffn2_norm[i], ffn2_w_in[i], ffn2_w_out[i])
    return x


def setup_inputs(seed: int = 0) -> dict:
    key = jax.random.key(seed)
    ks = jax.random.split(key, 24)
    f32 = jnp.float32

    def nrm(k, shape, scale):
        return jax.random.normal(k, shape, f32) * scale

    def gain(k, shape):
        return 1.0 + 0.05 * jax.random.normal(k, shape, f32)

    L, D, F = DEPTH, D_MODEL, D_FF
    return {
        "x_prompt": nrm(ks[0], (BATCH, SEQ, D), 1.0),
        "x_sample": nrm(ks[1], (DEC_BATCH, DEC_SEQ, D), 1.0),
        "ffn1_norm": gain(ks[2], (L, D)),
        "ffn1_w_in": nrm(ks[3], (L, D, 2 * F), D ** -0.5),
        "ffn1_w_out": nrm(ks[4], (L, F, D), F ** -0.5),
        "mix_norm": gain(ks[5], (L, D)),
        "w_in": nrm(ks[6], (L, D, IN_W), D ** -0.5),
        "b_gate": nrm(ks[7], (L, N_BRANCH * D), 0.01),
        "conv_w": nrm(ks[8], (L, 3, CONV_W), 3 ** -0.5),
        "pool_w": nrm(ks[9], (L, 4, POOL_GROUP, POOL_GROUP), POOL_GROUP ** -0.5),
        "pool_scale": 0.5 + 0.05 * jax.random.normal(ks[10], (L, POOL_W), f32),
        "attn_q_norm": gain(ks[11], (L, HEAD_DIM)),
        "attn_k_norm": gain(ks[12], (L, HEAD_DIM)),
        "diff_q_norm": gain(ks[13], (L, HEAD_DIM)),
        "diff_k_norm": gain(ks[14], (L, HEAD_DIM)),
        "diff_lambda": nrm(ks[15], (L, 4, HEAD_DIM), 0.1),
        "diff_out_norm": gain(ks[16], (L, DIFF_V)),
        "w_branch": nrm(ks[17], (L, N_BRANCH, BRANCH_W, D), BRANCH_W ** -0.5),
        "w_out": nrm(ks[18], (L, D, D), D ** -0.5),
        "ffn2_norm": gain(ks[19], (L, D)),
        "ffn2_w_in": nrm(ks[20], (L, D, 2 * F), D ** -0.5),
        "ffn2_w_out": nrm(ks[21], (L, F, D), F ** -0.5),
    }


def reference(x_prompt, x_sample, ffn1_norm, ffn1_w_in, ffn1_w_out, mix_norm, w_in, b_gate,
              conv_w, pool_w, pool_scale, attn_q_norm, attn_k_norm, diff_q_norm, diff_k_norm,
              diff_lambda, diff_out_norm, w_branch, w_out, ffn2_norm, ffn2_w_in, ffn2_w_out):
    y_prompt = trunk(x_prompt, ffn1_norm, ffn1_w_in, ffn1_w_out, mix_norm, w_in, b_gate, conv_w,
                     pool_w, pool_scale, attn_q_norm, attn_k_norm, diff_q_norm, diff_k_norm,
                     diff_lambda, diff_out_norm, w_branch, w_out, ffn2_norm, ffn2_w_in, ffn2_w_out)
    y_sample = trunk(x_sample, ffn1_norm, ffn1_w_in, ffn1_w_out, mix_norm, w_in, b_gate, conv_w,
                     pool_w, pool_scale, attn_q_norm, attn_k_norm, diff_q_norm, diff_k_norm,
                     diff_lambda, diff_out_norm, w_branch, w_out, ffn2_norm, ffn2_w_in, ffn2_w_out)
    return (y_prompt, y_sample)
```

```python
import functools
import math

import jax
import jax.numpy as jnp
import numpy as np
from jax import lax
from jax.experimental import pallas as pl
from jax.experimental.pallas import tpu as pltpu

F32 = jnp.float32
BF16 = jnp.bfloat16

D_MODEL = 1024
DEPTH = 2
GRID_W = 64
HEAD_DIM = 64
D_FF = 2816
N_BRANCH = 4
BRANCH_W = 512
POOL_WINDOWS = (2, 4, 8, 16)
POOL_GROUP = BRANCH_W // 4
GQA_HEADS = 8
GQA_KV = 2
DIFF_HEADS = 4
DIFF_V = 2 * HEAD_DIM
AXIAL_THETA = 10000.0
ROPE_THETA = 500000.0
ROPE_DIM = HEAD_DIM // 4
EPS = 1e-6

_OFF_CONV = 0
_OFF_POOL = 3 * BRANCH_W
_OFF_CQ = _OFF_POOL + BRANCH_W
_OFF_CK = _OFF_CQ + GQA_HEADS * HEAD_DIM
_OFF_CV = _OFF_CK + GQA_KV * HEAD_DIM
_OFF_DQ = _OFF_CV + GQA_KV * HEAD_DIM
_OFF_DK = _OFF_DQ + DIFF_HEADS * 2 * HEAD_DIM
_OFF_DV = _OFF_DK + DIFF_HEADS * 2 * HEAD_DIM
_OFF_GATE = _OFF_DV + DIFF_HEADS * DIFF_V
IN_W = _OFF_GATE + N_BRANCH * D_MODEL

LANES = 128
BF16_SUBLANES = 16
VMEM_LIMIT_BYTES = 56 * 1024 * 1024

HALO = BF16_SUBLANES
NEG_BIG = -1e30


def _tiles(seq):
    return dict(
        ffn_tm=512,
        proj_tm=256,
        merge_tm=256,
        gqa_tq=128,
        diff_tq=256,
        tkv=min(1024, seq),
        row_chunk=256,
    )


def _resident(shape):
    zeros = (0,) * len(shape)
    return pl.BlockSpec(shape, lambda *_: zeros, pipeline_mode=pl.Buffered(1))


def _params(n_axes):
    return pltpu.CompilerParams(dimension_semantics=("arbitrary",) * n_axes,
                                vmem_limit_bytes=VMEM_LIMIT_BYTES)


def _rms(xf, gain):
    return xf * lax.rsqrt(jnp.mean(xf * xf, axis=-1, keepdims=True) + EPS) * gain


def _ffn_kernel(x_ref, g_ref, win_ref, wout_ref, o_ref):
    x = x_ref[...]
    u = _rms(x, g_ref[...]).astype(BF16)
    h = jnp.dot(u, win_ref[...], preferred_element_type=F32)
    a = h[:, :D_FF]
    b = h[:, D_FF:]
    act = (a * jax.nn.sigmoid(a) * b).astype(BF16)
    y = jnp.dot(act, wout_ref[...], preferred_element_type=F32)
    o_ref[...] = x + 0.5 * y


def _ffn(x2, gain, w_in, w_out, tm):
    T = x2.shape[0]
    return pl.pallas_call(
        _ffn_kernel,
        grid=(T // tm,),
        in_specs=[pl.BlockSpec((tm, D_MODEL), lambda i: (i, 0)),
                  _resident((1, D_MODEL)),
                  _resident((D_MODEL, 2 * D_FF)),
                  _resident((D_FF, D_MODEL))],
        out_specs=pl.BlockSpec((tm, D_MODEL), lambda i: (i, 0)),
        out_shape=jax.ShapeDtypeStruct((T, D_MODEL), F32),
        compiler_params=_params(1),
        name="ffn",
    )(x2, gain, w_in, w_out)


def _head_norm_rope(z, bd, gain, cos, sin_a, sin_b, shift, scale):
    ss = jnp.dot((z * z).astype(BF16), bd, preferred_element_type=F32)
    zn = z * lax.rsqrt(ss * (1.0 / HEAD_DIM) + EPS) * gain
    out = zn * cos + pltpu.roll(zn, LANES - shift, 1) * sin_a + pltpu.roll(zn, shift, 1) * sin_b
    if scale != 1.0:
        out = out * scale
    return out.astype(BF16)


def _proj_kernel(x_ref, g_ref, w_ref, bgate_ref, bd_ref, qn_ref, kn_ref, dqn_ref, dkn_ref,
                 acos_ref, asa_ref, asb_ref, pcos_ref, psa_ref, psb_ref,
                 conv_o, pool_o, q_o, k_o, v_o, dq_o, dk_o, dv_o, gate_o):
    u = _rms(x_ref[...], g_ref[...]).astype(BF16)

    def proj(lo, width):
        return jnp.dot(u, w_ref[:, lo:lo + width], preferred_element_type=F32)

    conv_o[...] = proj(_OFF_CONV, 3 * BRANCH_W).astype(BF16)
    pool_o[...] = proj(_OFF_POOL, BRANCH_W).astype(BF16)

    bd = bd_ref[...]
    axial = (acos_ref[...], asa_ref[...], asb_ref[...], HEAD_DIM // 4)
    partial = (pcos_ref[...], psa_ref[...], psb_ref[...], ROPE_DIM // 2)
    q_scale = HEAD_DIM ** -0.5

    def heads(lo, width, out_ref, gain_ref, tables, scale):
        z = proj(lo, width)
        gain = gain_ref[...]
        for j in range(width // LANES):
            sl = slice(j * LANES, (j + 1) * LANES)
            out_ref[:, sl] = _head_norm_rope(z[:, sl], bd, gain, *tables, scale)

    heads(_OFF_CQ, GQA_HEADS * HEAD_DIM, q_o, qn_ref, axial, q_scale)
    heads(_OFF_CK, GQA_KV * HEAD_DIM, k_o, kn_ref, axial, 1.0)
    v_o[...] = proj(_OFF_CV, GQA_KV * HEAD_DIM).astype(BF16)
    heads(_OFF_DQ, DIFF_HEADS * 2 * HEAD_DIM, dq_o, dqn_ref, partial, q_scale)
    heads(_OFF_DK, DIFF_HEADS * 2 * HEAD_DIM, dk_o, dkn_ref, partial, 1.0)
    dv_o[...] = proj(_OFF_DV, DIFF_HEADS * DIFF_V).astype(BF16)

    for n in range(N_BRANCH):
        sl = slice(n * D_MODEL, (n + 1) * D_MODEL)
        g = proj(_OFF_GATE + n * D_MODEL, D_MODEL) + bgate_ref[:, sl]
        gate_o[:, sl] = jax.nn.sigmoid(g).astype(BF16)


def _proj(x2, lw, tabs, seq, tm):
    T = x2.shape[0]
    n_s = seq // tm
    tok = lambda w: pl.BlockSpec((tm, w), lambda i: (i, 0))
    tab = pl.BlockSpec((tm, LANES), lambda i: (i % n_s, 0))
    widths = (3 * BRANCH_W, BRANCH_W, GQA_HEADS * HEAD_DIM, GQA_KV * HEAD_DIM, GQA_KV * HEAD_DIM,
              DIFF_HEADS * 2 * HEAD_DIM, DIFF_HEADS * 2 * HEAD_DIM, DIFF_HEADS * DIFF_V,
              N_BRANCH * D_MODEL)
    return pl.pallas_call(
        _proj_kernel,
        grid=(T // tm,),
        in_specs=[tok(D_MODEL), _resident((1, D_MODEL)), _resident((D_MODEL, IN_W)),
                  _resident((1, N_BRANCH * D_MODEL)), _resident((LANES, LANES)),
                  _resident((1, LANES)), _resident((1, LANES)), _resident((1, LANES)),
                  _resident((1, LANES))] + [tab] * 6,
        out_specs=[tok(w) for w in widths],
        out_shape=[jax.ShapeDtypeStruct((T, w), BF16) for w in widths],
        compiler_params=_params(1),
        name="proj",
    )(x2, lw["mix_norm"], lw["w_in"], lw["b_gate"], lw["bd"], lw["q_norm"], lw["k_norm"],
      lw["dq_norm"], lw["dk_norm"], *tabs)


def _flash_rows(qs_ref, k_ref, v_ref, m_ref, l_ref, acc_ref, *, tkv, n_kv, row_chunk):
    R = qs_ref.shape[0]
    m_ref[...] = jnp.full(m_ref.shape, NEG_BIG, F32)
    l_ref[...] = jnp.zeros(l_ref.shape, F32)
    acc_ref[...] = jnp.zeros(acc_ref.shape, F32)

    def step(t, carry):
        start = pl.multiple_of(t * tkv, tkv)
        k = k_ref[0, pl.ds(start, tkv), :]
        v = v_ref[0, pl.ds(start, tkv), :]
        for c in range(R // row_chunk):
            rows = slice(c * row_chunk, (c + 1) * row_chunk)
            s = lax.dot_general(qs_ref[rows, :], k, (((1,), (1,)), ((), ())),
                                preferred_element_type=F32)
            m_prev = m_ref[rows, :]
            m_new = jnp.maximum(m_prev, jnp.max(s, axis=1, keepdims=True))
            alpha = jnp.exp(m_prev - m_new)
            p = jnp.exp(s - m_new)
            l_ref[rows, :] = alpha * l_ref[rows, :] + jnp.sum(p, axis=1, keepdims=True)
            acc_ref[rows, :] = alpha * acc_ref[rows, :] + jnp.dot(
                p.astype(BF16), v, preferred_element_type=F32)
            m_ref[rows, :] = m_new
        return carry

    lax.fori_loop(0, n_kv, step, 0)


def _gqa_kernel(q_ref, k_ref, v_ref, o_ref, qs_ref, m_ref, l_ref, acc_ref, *, tq, tkv, n_kv,
                row_chunk):
    low = lax.broadcasted_iota(jnp.int32, (tq, LANES), 1) < HEAD_DIM
    n_blk = GQA_HEADS // GQA_KV
    for j in range(n_blk):
        x = q_ref[0, :, j * LANES:(j + 1) * LANES]
        qs_ref[j * tq:(j + 1) * tq, :] = jnp.where(low, x, jnp.zeros_like(x))
        qs_ref[(n_blk + j) * tq:(n_blk + j + 1) * tq, :] = jnp.where(low, jnp.zeros_like(x), x)
    _flash_rows(qs_ref, k_ref, v_ref, m_ref, l_ref, acc_ref, tkv=tkv, n_kv=n_kv,
                row_chunk=row_chunk)
    for j in range(n_blk):
        r0 = slice(j * tq, (j + 1) * tq)
        r1 = slice((n_blk + j) * tq, (n_blk + j + 1) * tq)
        o0 = acc_ref[r0, :] / l_ref[r0, :]
        o1 = acc_ref[r1, :] / l_ref[r1, :]
        o_ref[0, :, j * LANES:(j + 1) * LANES] = jnp.where(low, o0, o1).astype(BF16)


def _gqa(q, k, v, t):
    B, S, W = q.shape
    tq, tkv = t["gqa_tq"], t["tkv"]
    R = GQA_HEADS * tq
    kern = functools.partial(_gqa_kernel, tq=tq, tkv=tkv, n_kv=S // tkv,
                             row_chunk=t["row_chunk"])
    return pl.pallas_call(
        kern,
        grid=(B, S // tq),
        in_specs=[pl.BlockSpec((1, tq, W), lambda b, i: (b, i, 0)),
                  pl.BlockSpec((1, S, LANES), lambda b, i: (b, 0, 0)),
                  pl.BlockSpec((1, S, LANES), lambda b, i: (b, 0, 0))],
        out_specs=pl.BlockSpec((1, tq, W), lambda b, i: (b, i, 0)),
        out_shape=jax.ShapeDtypeStruct((B, S, W), BF16),
        scratch_shapes=[pltpu.VMEM((R, LANES), BF16), pltpu.VMEM((R, 1), F32),
                        pltpu.VMEM((R, 1), F32), pltpu.VMEM((R, LANES), F32)],
        compiler_params=_params(2),
        name="gqa",
    )(q, k, v)


def _diff_kernel(lam_ref, on_ref, q_ref, k_ref, v_ref, o_ref, qs_ref, m_ref, l_ref, acc_ref, *,
                 tq, tkv, n_kv, row_chunk, lambda_init):
    low = lax.broadcasted_iota(jnp.int32, (tq, LANES), 1) < HEAD_DIM
    x = q_ref[0]
    qs_ref[0:tq, :] = jnp.where(low, x, jnp.zeros_like(x))
    qs_ref[tq:2 * tq, :] = jnp.where(low, jnp.zeros_like(x), x)
    _flash_rows(qs_ref, k_ref, v_ref, m_ref, l_ref, acc_ref, tkv=tkv, n_kv=n_kv,
                row_chunk=row_chunk)
    lv = lam_ref[...]
    lam = (jnp.exp(jnp.sum(lv[0:1] * lv[1:2], axis=1, keepdims=True))
           - jnp.exp(jnp.sum(lv[2:3] * lv[3:4], axis=1, keepdims=True)) + lambda_init)
    o = (acc_ref[0:tq, :] / l_ref[0:tq, :]
         - lam * (acc_ref[tq:2 * tq, :] / l_ref[tq:2 * tq, :]))
    o_ref[0] = (_rms(o, on_ref[...]) * (1.0 - lambda_init)).astype(BF16)


def _diff(q, k, v, lam, out_norm, lambda_init, t):
    B, S, W = q.shape
    tq, tkv = t["diff_tq"], t["tkv"]
    R = 2 * tq
    kern = functools.partial(_diff_kernel, tq=tq, tkv=tkv, n_kv=S // tkv,
                             row_chunk=t["row_chunk"], lambda_init=lambda_init)
    head = pl.BlockSpec((1, S, LANES), lambda b, h, i: (b, 0, h))
    return pl.pallas_call(
        kern,
        grid=(B, DIFF_HEADS, S // tq),
        in_specs=[pl.BlockSpec((4, HEAD_DIM), lambda b, h, i: (0, 0)),
                  pl.BlockSpec((1, DIFF_V), lambda b, h, i: (0, 0)),
                  pl.BlockSpec((1, tq, LANES), lambda b, h, i: (b, i, h)),
                  head, head],
        out_specs=pl.BlockSpec((1, tq, LANES), lambda b, h, i: (b, i, h)),
        out_shape=jax.ShapeDtypeStruct((B, S, W), BF16),
        scratch_shapes=[pltpu.VMEM((R, LANES), BF16), pltpu.VMEM((R, 1), F32),
                        pltpu.VMEM((R, 1), F32), pltpu.VMEM((R, LANES), F32)],
        compiler_params=_params(3),
        name="diff",
    )(lam, out_norm, q, k, v)


def _merge_kernel(x_ref, conv_ref, convp_ref, convn_ref, pool_ref, poolp_ref, pooln_ref,
                  og_ref, od_ref, gate_ref, cw_ref, pw_ref, ps_ref, wb_ref, wo_ref, o_ref,
                  cbuf, pbuf, *, tm, n_s, seq):
    si = pl.program_id(0) % n_s
    keep_prev = (si > 0).astype(F32)
    keep_next = (si < n_s - 1).astype(F32)
    W = BRANCH_W

    def gated(ref):
        blk = ref[...].astype(F32)
        return blk[:, 2 * W:3 * W] * blk[:, 0:W]
    cbuf[0:HALO, :] = gated(convp_ref) * keep_prev
    cbuf[HALO:HALO + tm, :] = gated(conv_ref)
    cbuf[HALO + tm:2 * HALO + tm, :] = gated(convn_ref) * keep_next
    cw = cw_ref[...]
    conv = (cw[0:1] * cbuf[HALO - 1:HALO - 1 + tm, :] + cw[1:2] * cbuf[HALO:HALO + tm, :]
            + cw[2:3] * cbuf[HALO + 1:HALO + 1 + tm, :])
    br_a = (conv_ref[:, W:2 * W].astype(F32) * conv).astype(BF16)

    pbuf[0:HALO, :] = poolp_ref[...].astype(F32) * keep_prev
    pbuf[HALO:HALO + tm, :] = pool_ref[...].astype(F32)
    pbuf[HALO + tm:2 * HALO + tm, :] = pooln_ref[...].astype(F32) * keep_next
    pos = si * tm + lax.broadcasted_iota(jnp.int32, (tm, 1), 0)
    pooled = []
    for gi, win in enumerate(POOL_WINDOWS):
        lanes = slice(gi * POOL_GROUP, (gi + 1) * POOL_GROUP)
        half = win // 2
        tot = pbuf[HALO - half:HALO - half + tm, lanes]
        for off in range(-half + 1, half):
            tot = tot + pbuf[HALO + off:HALO + off + tm, lanes]
        cnt = (jnp.clip(pos - half + win, 0, seq) - jnp.clip(pos - half, 0, seq)).astype(F32)
        mean_diff = (tot / cnt - pbuf[HALO:HALO + tm, lanes]).astype(BF16)
        pooled.append(jnp.dot(mean_diff, pw_ref[gi], preferred_element_type=F32))
    br_b = (jnp.concatenate(pooled, axis=1) * ps_ref[...]).astype(BF16)

    merged = None
    for n, br in enumerate((br_a, br_b, og_ref[...], od_ref[...])):
        y = jnp.dot(br, wb_ref[n], preferred_element_type=F32)
        y = gate_ref[:, n * D_MODEL:(n + 1) * D_MODEL].astype(F32) * y
        merged = y if merged is None else merged + y
    o_ref[...] = x_ref[...] + jnp.dot(merged.astype(BF16), wo_ref[...],
                                      preferred_element_type=F32)


def _merge(x2, conv, pool, og, od, gates, lw, seq, tm):
    T = x2.shape[0]
    n_s = seq // tm
    per = tm // HALO
    last = T // HALO - 1
    tok = lambda w: pl.BlockSpec((tm, w), lambda i: (i, 0))
    prev = lambda w: pl.BlockSpec((HALO, w), lambda i: (jnp.maximum(i * per - 1, 0), 0))
    nxt = lambda w: pl.BlockSpec((HALO, w), lambda i: (jnp.minimum((i + 1) * per, last), 0))
    kern = functools.partial(_merge_kernel, tm=tm, n_s=n_s, seq=seq)
    return pl.pallas_call(
        kern,
        grid=(T // tm,),
        in_specs=[tok(D_MODEL),
                  tok(3 * BRANCH_W), prev(3 * BRANCH_W), nxt(3 * BRANCH_W),
                  tok(BRANCH_W), prev(BRANCH_W), nxt(BRANCH_W),
                  tok(BRANCH_W), tok(BRANCH_W), tok(N_BRANCH * D_MODEL),
                  _resident((3, BRANCH_W)), _resident((4, POOL_GROUP, POOL_GROUP)),
                  _resident((1, BRANCH_W)), _resident((N_BRANCH, BRANCH_W, D_MODEL)),
                  _resident((D_MODEL, D_MODEL))],
        out_specs=tok(D_MODEL),
        out_shape=jax.ShapeDtypeStruct((T, D_MODEL), F32),
        scratch_shapes=[pltpu.VMEM((tm + 2 * HALO, BRANCH_W), F32),
                        pltpu.VMEM((tm + 2 * HALO, BRANCH_W), F32)],
        compiler_params=_params(1),
        name="merge",
    )(x2, conv, conv, conv, pool, pool, pool, og, od, gates,
      lw["conv_w"], lw["pool_w"], lw["pool_scale"], lw["w_branch"], lw["w_out"])


def _gqa_lane_perm():
    n_blk = GQA_HEADS // GQA_KV
    perm = np.empty(GQA_HEADS * HEAD_DIM, np.int32)
    for j in range(n_blk):
        for g in range(GQA_KV):
            dst = j * LANES + g * HEAD_DIM
            src = (g * n_blk + j) * HEAD_DIM
            perm[dst:dst + HEAD_DIM] = np.arange(src, src + HEAD_DIM)
    return perm


def _rope_tables(seq):
    def freqs(n, theta):
        half = n // 2
        return jnp.exp(-math.log(theta) * jnp.arange(half, dtype=F32) * (2.0 / n))

    def angles(pos, n, theta):
        return pos.astype(F32)[:, None] * freqs(n, theta)[None, :]

    def rotary(ang):
        c, s, z = jnp.cos(ang), jnp.sin(ang), jnp.zeros_like(ang)
        return (jnp.concatenate([c, c], 1), jnp.concatenate([-s, z], 1),
                jnp.concatenate([z, s], 1))

    rows = seq // GRID_W
    row = jnp.repeat(jnp.arange(rows), GRID_W)
    col = jnp.tile(jnp.arange(GRID_W), rows)
    half_dim = HEAD_DIM // 2
    ax = [jnp.concatenate([a, b], 1) for a, b in zip(rotary(angles(row, half_dim, AXIAL_THETA)),
                                                     rotary(angles(col, half_dim, AXIAL_THETA)))]
    rest = HEAD_DIM - ROPE_DIM
    pr = rotary(angles(jnp.arange(seq), ROPE_DIM, ROPE_THETA))
    pt = [jnp.concatenate([pr[0], jnp.ones((seq, rest), F32)], 1),
          jnp.concatenate([pr[1], jnp.zeros((seq, rest), F32)], 1),
          jnp.concatenate([pr[2], jnp.zeros((seq, rest), F32)], 1)]
    two = lambda a: jnp.concatenate([a, a], 1)
    return tuple(two(a) for a in ax) + tuple(two(a) for a in pt)


def _layer_weights(i, ffn1_norm, ffn1_w_in, ffn1_w_out, mix_norm, w_in, b_gate, conv_w, pool_w,
                   pool_scale, attn_q_norm, attn_k_norm, diff_q_norm, diff_k_norm, diff_lambda,
                   diff_out_norm, w_branch, w_out, ffn2_norm, ffn2_w_in, ffn2_w_out):
    perm = _gqa_lane_perm()
    wi = w_in[i]
    wi = wi.at[:, _OFF_CQ:_OFF_CK].set(wi[:, _OFF_CQ:_OFF_CK][:, perm])
    wb = w_branch[i]
    wb = wb.at[2].set(wb[2][perm, :])
    two = lambda g: jnp.concatenate([g, g]).reshape(1, LANES).astype(F32)
    blk = np.arange(LANES) // HEAD_DIM
    return dict(
        ffn1=(ffn1_norm[i].reshape(1, D_MODEL), ffn1_w_in[i].astype(BF16),
              ffn1_w_out[i].astype(BF16)),
        ffn2=(ffn2_norm[i].reshape(1, D_MODEL), ffn2_w_in[i].astype(BF16),
              ffn2_w_out[i].astype(BF16)),
        mix_norm=mix_norm[i].reshape(1, D_MODEL),
        w_in=wi.astype(BF16),
        b_gate=b_gate[i].reshape(1, N_BRANCH * D_MODEL),
        bd=jnp.asarray(blk[:, None] == blk[None, :], BF16),
        q_norm=two(attn_q_norm[i]), k_norm=two(attn_k_norm[i]),
        dq_norm=two(diff_q_norm[i]), dk_norm=two(diff_k_norm[i]),
        diff_lambda=diff_lambda[i].astype(F32),
        diff_out_norm=diff_out_norm[i].reshape(1, DIFF_V),
        lambda_init=0.8 - 0.6 * math.exp(-0.3 * i),
        conv_w=conv_w[i], pool_w=pool_w[i].astype(BF16),
        pool_scale=pool_scale[i].reshape(1, BRANCH_W),
        w_branch=wb.astype(BF16), w_out=w_out[i].astype(BF16),
    )


def _trunk(x, layers):
    B, S, _ = x.shape
    T = B * S
    t = _tiles(S)
    tabs = _rope_tables(S)
    x2 = x.reshape(T, D_MODEL)
    for lw in layers:
        x2 = _ffn(x2, *lw["ffn1"], t["ffn_tm"])
        conv, pool, q, k, v, dq, dk, dv, gates = _proj(x2, lw, tabs, S, t["proj_tm"])
        seq3 = lambda a: a.reshape(B, S, a.shape[-1])
        og = _gqa(seq3(q), seq3(k), seq3(v), t)
        od = _diff(seq3(dq), seq3(dk), seq3(dv), lw["diff_lambda"], lw["diff_out_norm"],
                   lw["lambda_init"], t)
        x2 = _merge(x2, conv, pool, og.reshape(T, -1), od.reshape(T, -1), gates, lw, S,
                    t["merge_tm"])
        x2 = _ffn(x2, *lw["ffn2"], t["ffn_tm"])
    return x2.reshape(B, S, D_MODEL)


def kernel(x_prompt, x_sample, ffn1_norm, ffn1_w_in, ffn1_w_out, mix_norm, w_in, b_gate, conv_w, pool_w, pool_scale, attn_q_norm, attn_k_norm, diff_q_norm, diff_k_norm, diff_lambda, diff_out_norm, w_branch, w_out, ffn2_norm, ffn2_w_in, ffn2_w_out):
    weights = (ffn1_norm, ffn1_w_in, ffn1_w_out, mix_norm, w_in, b_gate, conv_w, pool_w,
               pool_scale, attn_q_norm, attn_k_norm, diff_q_norm, diff_k_norm, diff_lambda,
               diff_out_norm, w_branch, w_out, ffn2_norm, ffn2_w_in, ffn2_w_out)
    layers = [_layer_weights(i, *weights) for i in range(DEPTH)]
    return (_trunk(x_prompt, layers), _trunk(x_sample, layers))
```

```python
import functools
import math

import jax
import jax.numpy as jnp
import numpy as np
from jax import lax
from jax.experimental import pallas as pl
from jax.experimental.pallas import tpu as pltpu

F32 = jnp.float32
BF16 = jnp.bfloat16

D_MODEL = 1024
DEPTH = 2
GRID_W = 64
HEAD_DIM = 64
D_FF = 2816
N_BRANCH = 4
BRANCH_W = 512
POOL_WINDOWS = (2, 4, 8, 16)
POOL_GROUP = BRANCH_W // 4
GQA_HEADS = 8
GQA_KV = 2
DIFF_HEADS = 4
DIFF_V = 2 * HEAD_DIM
AXIAL_THETA = 10000.0
ROPE_THETA = 500000.0
ROPE_DIM = HEAD_DIM // 4
EPS = 1e-6

_OFF_CONV = 0
_OFF_POOL = 3 * BRANCH_W
_OFF_CQ = _OFF_POOL + BRANCH_W
_OFF_CK = _OFF_CQ + GQA_HEADS * HEAD_DIM
_OFF_CV = _OFF_CK + GQA_KV * HEAD_DIM
_OFF_DQ = _OFF_CV + GQA_KV * HEAD_DIM
_OFF_DK = _OFF_DQ + DIFF_HEADS * 2 * HEAD_DIM
_OFF_DV = _OFF_DK + DIFF_HEADS * 2 * HEAD_DIM
_OFF_GATE = _OFF_DV + DIFF_HEADS * DIFF_V
IN_W = _OFF_GATE + N_BRANCH * D_MODEL

LANES = 128
BF16_SUBLANES = 16
VMEM_LIMIT_BYTES = 56 * 1024 * 1024

HALO = BF16_SUBLANES
VT_ROWS = LANES + BF16_SUBLANES
NEG_BIG = -1e30


def _tiles(seq):
    return dict(
        ffn_tm=512,
        proj_tm=256,
        merge_tm=256,
        gqa_tq=128,
        diff_tq=512,
        tkv=min(1024, seq // 2),
    )


def _resident(shape):
    zeros = (0,) * len(shape)
    return pl.BlockSpec(shape, lambda *_: zeros, pipeline_mode=pl.Buffered(1))


def _params(n_axes):
    return pltpu.CompilerParams(dimension_semantics=("arbitrary",) * n_axes,
                                vmem_limit_bytes=VMEM_LIMIT_BYTES)


def _rms(xf, gain):
    return xf * lax.rsqrt(jnp.mean(xf * xf, axis=-1, keepdims=True) + EPS) * gain


def _ffn_kernel(x_ref, g_ref, win_ref, wout_ref, o_ref):
    x = x_ref[...]
    u = _rms(x, g_ref[...]).astype(BF16)
    h = jnp.dot(u, win_ref[...], preferred_element_type=F32)
    a = h[:, :D_FF]
    b = h[:, D_FF:]
    act = (a * jax.nn.sigmoid(a) * b).astype(BF16)
    y = jnp.dot(act, wout_ref[...], preferred_element_type=F32)
    o_ref[...] = x + 0.5 * y


def _ffn(x2, gain, w_in, w_out, tm):
    T = x2.shape[0]
    return pl.pallas_call(
        _ffn_kernel,
        grid=(T // tm,),
        in_specs=[pl.BlockSpec((tm, D_MODEL), lambda i: (i, 0)),
                  _resident((1, D_MODEL)),
                  _resident((D_MODEL, 2 * D_FF)),
                  _resident((D_FF, D_MODEL))],
        out_specs=pl.BlockSpec((tm, D_MODEL), lambda i: (i, 0)),
        out_shape=jax.ShapeDtypeStruct((T, D_MODEL), F32),
        compiler_params=_params(1),
        name="ffn",
    )(x2, gain, w_in, w_out)


def _head_norm_rope(z, bd, gain, cos, sin_a, sin_b, shift, scale):
    ss = jnp.dot((z * z).astype(BF16), bd, preferred_element_type=F32)
    zn = z * lax.rsqrt(ss * (1.0 / HEAD_DIM) + EPS) * gain
    out = zn * cos + pltpu.roll(zn, LANES - shift, 1) * sin_a + pltpu.roll(zn, shift, 1) * sin_b
    if scale != 1.0:
        out = out * scale
    return out.astype(BF16)


def _proj_kernel(x_ref, g_ref, w_ref, bgate_ref, bd_ref, qn_ref, kn_ref, dqn_ref, dkn_ref,
                 acos_ref, asa_ref, asb_ref, pcos_ref, psa_ref, psb_ref,
                 conv_o, pool_o, q_o, k_o, vt_o, dq_o, dk_o, dvt_o, gate_o):
    u = _rms(x_ref[...], g_ref[...]).astype(BF16)

    def proj(lo, width):
        return jnp.dot(u, w_ref[:, lo:lo + width], preferred_element_type=F32)

    conv_o[...] = proj(_OFF_CONV, 3 * BRANCH_W).astype(BF16)
    pool_o[...] = proj(_OFF_POOL, BRANCH_W).astype(BF16)

    bd = bd_ref[...]
    axial = (acos_ref[...], asa_ref[...], asb_ref[...], HEAD_DIM // 4)
    partial = (pcos_ref[...], psa_ref[...], psb_ref[...], ROPE_DIM // 2)
    q_scale = HEAD_DIM ** -0.5 * math.log2(math.e)

    def heads(lo, width, out_ref, gain_ref, tables, scale):
        z = proj(lo, width)
        gain = gain_ref[...]
        for j in range(width // LANES):
            sl = slice(j * LANES, (j + 1) * LANES)
            out_ref[:, sl] = _head_norm_rope(z[:, sl], bd, gain, *tables, scale)

    heads(_OFF_CQ, GQA_HEADS * HEAD_DIM, q_o, qn_ref, axial, q_scale)
    heads(_OFF_CK, GQA_KV * HEAD_DIM, k_o, kn_ref, axial, 1.0)
    def values_t(lo, n_heads, out_ref):
        vt = proj(lo, n_heads * LANES).T.astype(BF16)
        ones = jnp.ones((VT_ROWS - LANES, vt.shape[1]), BF16)
        for h in range(n_heads):
            out_ref[h * VT_ROWS:h * VT_ROWS + LANES, :] = vt[h * LANES:(h + 1) * LANES, :]
            out_ref[h * VT_ROWS + LANES:(h + 1) * VT_ROWS, :] = ones

    values_t(_OFF_CV, 1, vt_o)
    heads(_OFF_DQ, DIFF_HEADS * 2 * HEAD_DIM, dq_o, dqn_ref, partial, q_scale)
    heads(_OFF_DK, DIFF_HEADS * 2 * HEAD_DIM, dk_o, dkn_ref, partial, 1.0)
    values_t(_OFF_DV, DIFF_HEADS, dvt_o)

    for n in range(N_BRANCH):
        sl = slice(n * D_MODEL, (n + 1) * D_MODEL)
        g = proj(_OFF_GATE + n * D_MODEL, D_MODEL) + bgate_ref[:, sl]
        gate_o[:, sl] = jax.nn.sigmoid(g).astype(BF16)


def _proj(x2, lw, tabs, seq, tm):
    T = x2.shape[0]
    n_s = seq // tm
    tok = lambda w: pl.BlockSpec((tm, w), lambda i: (i, 0))
    tab = pl.BlockSpec((tm, LANES), lambda i: (i % n_s, 0))
    tok_t = lambda w: pl.BlockSpec((w, tm), lambda i: (0, i))
    outs = ((3 * BRANCH_W, False), (BRANCH_W, False), (GQA_HEADS * HEAD_DIM, False),
            (GQA_KV * HEAD_DIM, False), (VT_ROWS, True),
            (DIFF_HEADS * 2 * HEAD_DIM, False), (DIFF_HEADS * 2 * HEAD_DIM, False),
            (DIFF_HEADS * VT_ROWS, True), (N_BRANCH * D_MODEL, False))
    return pl.pallas_call(
        _proj_kernel,
        grid=(T // tm,),
        in_specs=[tok(D_MODEL), _resident((1, D_MODEL)), _resident((D_MODEL, IN_W)),
                  _resident((1, N_BRANCH * D_MODEL)), _resident((LANES, LANES)),
                  _resident((1, LANES)), _resident((1, LANES)), _resident((1, LANES)),
                  _resident((1, LANES))] + [tab] * 6,
        out_specs=[tok_t(w) if tr else tok(w) for w, tr in outs],
        out_shape=[jax.ShapeDtypeStruct((w, T) if tr else (T, w), BF16) for w, tr in outs],
        compiler_params=_params(1),
        name="proj",
    )(x2, lw["mix_norm"], lw["w_in"], lw["b_gate"], lw["bd"], lw["q_norm"], lw["k_norm"],
      lw["dq_norm"], lw["dk_norm"], *tabs)


def _flash_cols(qst_ref, k_ref, vt_ref, m_ref, acc_ref, s_ref, smax_ref, alpha_ref, p_ref,
                *, tkv, n_kv):
    m_ref[...] = jnp.full(m_ref.shape, NEG_BIG, F32)
    acc_ref[...] = jnp.zeros(acc_ref.shape, F32)

    def kv_start(u):
        return u * tkv if isinstance(u, int) else pl.multiple_of(u * tkv, tkv)

    def scores(u, par):
        k = k_ref[0, pl.ds(kv_start(u), tkv), :]
        s = jnp.dot(k, qst_ref[...], preferred_element_type=F32)
        s_ref[par] = s
        smax_ref[par] = jnp.max(s, axis=0, keepdims=True)

    def softmax(par):
        m_prev = m_ref[...]
        m_new = jnp.maximum(m_prev, smax_ref[par])
        alpha_ref[par] = jnp.exp2(m_prev - m_new)
        p_ref[par] = jnp.exp2(s_ref[par] - m_new).astype(BF16)
        m_ref[...] = m_new

    def pv(u, par):
        vt = vt_ref[:, pl.ds(kv_start(u), tkv)]
        acc_ref[...] = alpha_ref[par] * acc_ref[...] + jnp.dot(
            vt, p_ref[par], preferred_element_type=F32)

    scores(0, 0)
    scores(1, 1)
    softmax(0)

    def slot_pair(i, carry):
        u = 2 * i
        scores(u, 0)
        softmax(1)
        pv(u - 2, 0)
        scores(u + 1, 1)
        softmax(0)
        pv(u - 1, 1)
        return carry

    lax.fori_loop(1, n_kv // 2, slot_pair, 0)
    softmax(1)
    pv(n_kv - 2, 0)
    pv(n_kv - 1, 1)


def _stack_queries(x, qst_ref, lo_cols, hi_cols):
    xt = x.astype(F32).T
    top = lax.broadcasted_iota(jnp.int32, xt.shape, 0) < HEAD_DIM
    zero = jnp.zeros_like(xt)
    qst_ref[:, lo_cols] = jnp.where(top, xt, zero).astype(BF16)
    qst_ref[:, hi_cols] = jnp.where(top, zero, xt).astype(BF16)


def _attn_scratch(R, tkv):
    return [pltpu.VMEM((LANES, R), BF16), pltpu.VMEM((1, R), F32),
            pltpu.VMEM((VT_ROWS, R), F32), pltpu.VMEM((2, tkv, R), F32),
            pltpu.VMEM((2, 1, R), F32), pltpu.VMEM((2, 1, R), F32),
            pltpu.VMEM((2, tkv, R), BF16)]


def _gqa_kernel(q_ref, k_ref, vt_ref, o_ref, qst_ref, m_ref, acc_ref, *pipe, tq, tkv, n_kv):
    n_blk = GQA_HEADS // GQA_KV
    col = lambda h: slice(h * tq, (h + 1) * tq)
    for j in range(n_blk):
        _stack_queries(q_ref[0, :, j * LANES:(j + 1) * LANES], qst_ref, col(j), col(n_blk + j))
    _flash_cols(qst_ref, k_ref, vt_ref, m_ref, acc_ref, *pipe, tkv=tkv, n_kv=n_kv)
    den = slice(LANES, LANES + 1)
    for j in range(n_blk):
        c0, c1 = col(j), col(n_blk + j)
        top = acc_ref[0:HEAD_DIM, c0] / acc_ref[den, c0]
        bot = acc_ref[HEAD_DIM:2 * HEAD_DIM, c1] / acc_ref[den, c1]
        o_ref[0, :, j * LANES:(j + 1) * LANES] = (
            jnp.concatenate([top, bot], axis=0).T.astype(BF16))


def _gqa(q, k, vt, t):
    B, S, W = q.shape
    tq, tkv = t["gqa_tq"], t["tkv"]
    kern = functools.partial(_gqa_kernel, tq=tq, tkv=tkv, n_kv=S // tkv)
    return pl.pallas_call(
        kern,
        grid=(B, S // tq),
        in_specs=[pl.BlockSpec((1, tq, W), lambda b, i: (b, i, 0)),
                  pl.BlockSpec((1, S, LANES), lambda b, i: (b, 0, 0)),
                  pl.BlockSpec((VT_ROWS, S), lambda b, i: (0, b))],
        out_specs=pl.BlockSpec((1, tq, W), lambda b, i: (b, i, 0)),
        out_shape=jax.ShapeDtypeStruct((B, S, W), BF16),
        scratch_shapes=_attn_scratch(GQA_HEADS * tq, tkv),
        compiler_params=_params(2),
        name="gqa",
    )(q, k, vt)


def _diff_kernel(lam_ref, on_ref, q_ref, k_ref, vt_ref, o_ref, qst_ref, m_ref, acc_ref,
                 *pipe, tq, tkv, n_kv, lambda_init):
    c0, c1 = slice(0, tq), slice(tq, 2 * tq)
    _stack_queries(q_ref[0], qst_ref, c0, c1)
    _flash_cols(qst_ref, k_ref, vt_ref, m_ref, acc_ref, *pipe, tkv=tkv, n_kv=n_kv)
    lv = lam_ref[...]
    lam = (jnp.exp(jnp.sum(lv[0:1] * lv[1:2], axis=1, keepdims=True))
           - jnp.exp(jnp.sum(lv[2:3] * lv[3:4], axis=1, keepdims=True)) + lambda_init)
    out, den = slice(0, LANES), slice(LANES, LANES + 1)
    ot = (acc_ref[out, c0] / acc_ref[den, c0]
          - lam * (acc_ref[out, c1] / acc_ref[den, c1]))
    o_ref[0] = (_rms(ot.T, on_ref[...]) * (1.0 - lambda_init)).astype(BF16)


def _diff(q, k, vt, lam, out_norm, lambda_init, t):
    B, S, W = q.shape
    tq, tkv = t["diff_tq"], t["tkv"]
    kern = functools.partial(_diff_kernel, tq=tq, tkv=tkv, n_kv=S // tkv,
                             lambda_init=lambda_init)
    return pl.pallas_call(
        kern,
        grid=(B, DIFF_HEADS, S // tq),
        in_specs=[pl.BlockSpec((4, HEAD_DIM), lambda b, h, i: (0, 0)),
                  pl.BlockSpec((1, DIFF_V), lambda b, h, i: (0, 0)),
                  pl.BlockSpec((1, tq, LANES), lambda b, h, i: (b, i, h)),
                  pl.BlockSpec((1, S, LANES), lambda b, h, i: (b, 0, h)),
                  pl.BlockSpec((VT_ROWS, S), lambda b, h, i: (h, b))],
        out_specs=pl.BlockSpec((1, tq, LANES), lambda b, h, i: (b, i, h)),
        out_shape=jax.ShapeDtypeStruct((B, S, W), BF16),
        scratch_shapes=_attn_scratch(2 * tq, tkv),
        compiler_params=_params(3),
        name="diff",
    )(lam, out_norm, q, k, vt)


def _merge_kernel(x_ref, conv_ref, convp_ref, convn_ref, pool_ref, poolp_ref, pooln_ref,
                  og_ref, od_ref, gate_ref, cw_ref, pw_ref, ps_ref, wb_ref, wo_ref, o_ref,
                  cbuf, pbuf, *, tm, n_s, seq):
    si = pl.program_id(0) % n_s
    keep_prev = (si > 0).astype(F32)
    keep_next = (si < n_s - 1).astype(F32)
    W = BRANCH_W

    def gated(ref):
        blk = ref[...].astype(F32)
        return blk[:, 2 * W:3 * W] * blk[:, 0:W]
    cbuf[0:HALO, :] = gated(convp_ref) * keep_prev
    cbuf[HALO:HALO + tm, :] = gated(conv_ref)
    cbuf[HALO + tm:2 * HALO + tm, :] = gated(convn_ref) * keep_next
    cw = cw_ref[...]
    conv = (cw[0:1] * cbuf[HALO - 1:HALO - 1 + tm, :] + cw[1:2] * cbuf[HALO:HALO + tm, :]
            + cw[2:3] * cbuf[HALO + 1:HALO + 1 + tm, :])
    br_a = (conv_ref[:, W:2 * W].astype(F32) * conv).astype(BF16)

    pbuf[0:HALO, :] = poolp_ref[...].astype(F32) * keep_prev
    pbuf[HALO:HALO + tm, :] = pool_ref[...].astype(F32)
    pbuf[HALO + tm:2 * HALO + tm, :] = pooln_ref[...].astype(F32) * keep_next
    pos = si * tm + lax.broadcasted_iota(jnp.int32, (tm, 1), 0)
    pooled = []
    for gi, win in enumerate(POOL_WINDOWS):
        lanes = slice(gi * POOL_GROUP, (gi + 1) * POOL_GROUP)
        half = win // 2
        tot = pbuf[HALO - half:HALO - half + tm, lanes]
        for off in range(-half + 1, half):
            tot = tot + pbuf[HALO + off:HALO + off + tm, lanes]
        cnt = (jnp.clip(pos - half + win, 0, seq) - jnp.clip(pos - half, 0, seq)).astype(F32)
        mean_diff = (tot / cnt - pbuf[HALO:HALO + tm, lanes]).astype(BF16)
        pooled.append(jnp.dot(mean_diff, pw_ref[gi], preferred_element_type=F32))
    br_b = (jnp.concatenate(pooled, axis=1) * ps_ref[...]).astype(BF16)

    merged = None
    for n, br in enumerate((br_a, br_b, og_ref[...], od_ref[...])):
        y = jnp.dot(br, wb_ref[n], preferred_element_type=F32)
        y = gate_ref[:, n * D_MODEL:(n + 1) * D_MODEL].astype(F32) * y
        merged = y if merged is None else merged + y
    o_ref[...] = x_ref[...] + jnp.dot(merged.astype(BF16), wo_ref[...],
                                      preferred_element_type=F32)


def _merge(x2, conv, pool, og, od, gates, lw, seq, tm):
    T = x2.shape[0]
    n_s = seq // tm
    per = tm // HALO
    last = T // HALO - 1
    tok = lambda w: pl.BlockSpec((tm, w), lambda i: (i, 0))
    prev = lambda w: pl.BlockSpec((HALO, w), lambda i: (jnp.maximum(i * per - 1, 0), 0))
    nxt = lambda w: pl.BlockSpec((HALO, w), lambda i: (jnp.minimum((i + 1) * per, last), 0))
    kern = functools.partial(_merge_kernel, tm=tm, n_s=n_s, seq=seq)
    return pl.pallas_call(
        kern,
        grid=(T // tm,),
        in_specs=[tok(D_MODEL),
                  tok(3 * BRANCH_W), prev(3 * BRANCH_W), nxt(3 * BRANCH_W),
                  tok(BRANCH_W), prev(BRANCH_W), nxt(BRANCH_W),
                  tok(BRANCH_W), tok(BRANCH_W), tok(N_BRANCH * D_MODEL),
                  _resident((3, BRANCH_W)), _resident((4, POOL_GROUP, POOL_GROUP)),
                  _resident((1, BRANCH_W)), _resident((N_BRANCH, BRANCH_W, D_MODEL)),
                  _resident((D_MODEL, D_MODEL))],
        out_specs=tok(D_MODEL),
        out_shape=jax.ShapeDtypeStruct((T, D_MODEL), F32),
        scratch_shapes=[pltpu.VMEM((tm + 2 * HALO, BRANCH_W), F32),
                        pltpu.VMEM((tm + 2 * HALO, BRANCH_W), F32)],
        compiler_params=_params(1),
        name="merge",
    )(x2, conv, conv, conv, pool, pool, pool, og, od, gates,
      lw["conv_w"], lw["pool_w"], lw["pool_scale"], lw["w_branch"], lw["w_out"])


def _gqa_lane_perm():
    n_blk = GQA_HEADS // GQA_KV
    perm = np.empty(GQA_HEADS * HEAD_DIM, np.int32)
    for j in range(n_blk):
        for g in range(GQA_KV):
            dst = j * LANES + g * HEAD_DIM
            src = (g * n_blk + j) * HEAD_DIM
            perm[dst:dst + HEAD_DIM] = np.arange(src, src + HEAD_DIM)
    return perm


def _rope_tables(seq):
    def freqs(n, theta):
        half = n // 2
        return jnp.exp(-math.log(theta) * jnp.arange(half, dtype=F32) * (2.0 / n))

    def angles(pos, n, theta):
        return pos.astype(F32)[:, None] * freqs(n, theta)[None, :]

    def rotary(ang):
        c, s, z = jnp.cos(ang), jnp.sin(ang), jnp.zeros_like(ang)
        return (jnp.concatenate([c, c], 1), jnp.concatenate([-s, z], 1),
                jnp.concatenate([z, s], 1))

    rows = seq // GRID_W
    row = jnp.repeat(jnp.arange(rows), GRID_W)
    col = jnp.tile(jnp.arange(GRID_W), rows)
    half_dim = HEAD_DIM // 2
    ax = [jnp.concatenate([a, b], 1) for a, b in zip(rotary(angles(row, half_dim, AXIAL_THETA)),
                                                     rotary(angles(col, half_dim, AXIAL_THETA)))]
    rest = HEAD_DIM - ROPE_DIM
    pr = rotary(angles(jnp.arange(seq), ROPE_DIM, ROPE_THETA))
    pt = [jnp.concatenate([pr[0], jnp.ones((seq, rest), F32)], 1),
          jnp.concatenate([pr[1], jnp.zeros((seq, rest), F32)], 1),
          jnp.concatenate([pr[2], jnp.zeros((seq, rest), F32)], 1)]
    two = lambda a: jnp.concatenate([a, a], 1)
    return tuple(two(a) for a in ax) + tuple(two(a) for a in pt)


def _layer_weights(i, ffn1_norm, ffn1_w_in, ffn1_w_out, mix_norm, w_in, b_gate, conv_w, pool_w,
                   pool_scale, attn_q_norm, attn_k_norm, diff_q_norm, diff_k_norm, diff_lambda,
                   diff_out_norm, w_branch, w_out, ffn2_norm, ffn2_w_in, ffn2_w_out):
    perm = _gqa_lane_perm()
    wi = w_in[i]
    wi = wi.at[:, _OFF_CQ:_OFF_CK].set(wi[:, _OFF_CQ:_OFF_CK][:, perm])
    wb = w_branch[i]
    wb = wb.at[2].set(wb[2][perm, :])
    two = lambda g: jnp.concatenate([g, g]).reshape(1, LANES).astype(F32)
    blk = np.arange(LANES) // HEAD_DIM
    return dict(
        ffn1=(ffn1_norm[i].reshape(1, D_MODEL), ffn1_w_in[i].astype(BF16),
              ffn1_w_out[i].astype(BF16)),
        ffn2=(ffn2_norm[i].reshape(1, D_MODEL), ffn2_w_in[i].astype(BF16),
              ffn2_w_out[i].astype(BF16)),
        mix_norm=mix_norm[i].reshape(1, D_MODEL),
        w_in=wi.astype(BF16),
        b_gate=b_gate[i].reshape(1, N_BRANCH * D_MODEL),
        bd=jnp.asarray(blk[:, None] == blk[None, :], BF16),
        q_norm=two(attn_q_norm[i]), k_norm=two(attn_k_norm[i]),
        dq_norm=two(diff_q_norm[i]), dk_norm=two(diff_k_norm[i]),
        diff_lambda=diff_lambda[i].astype(F32),
        diff_out_norm=diff_out_norm[i].reshape(1, DIFF_V),
        lambda_init=0.8 - 0.6 * math.exp(-0.3 * i),
        conv_w=conv_w[i], pool_w=pool_w[i].astype(BF16),
        pool_scale=pool_scale[i].reshape(1, BRANCH_W),
        w_branch=wb.astype(BF16), w_out=w_out[i].astype(BF16),
    )


def _trunk(x, layers):
    B, S, _ = x.shape
    T = B * S
    t = _tiles(S)
    tabs = _rope_tables(S)
    x2 = x.reshape(T, D_MODEL)
    for lw in layers:
        x2 = _ffn(x2, *lw["ffn1"], t["ffn_tm"])
        conv, pool, q, k, vt, dq, dk, dvt, gates = _proj(x2, lw, tabs, S, t["proj_tm"])
        seq3 = lambda a: a.reshape(B, S, a.shape[-1])
        og = _gqa(seq3(q), seq3(k), vt, t)
        od = _diff(seq3(dq), seq3(dk), dvt, lw["diff_lambda"], lw["diff_out_norm"],
                   lw["lambda_init"], t)
        x2 = _merge(x2, conv, pool, og.reshape(T, -1), od.reshape(T, -1), gates, lw, S,
                    t["merge_tm"])
        x2 = _ffn(x2, *lw["ffn2"], t["ffn_tm"])
    return x2.reshape(B, S, D_MODEL)


def kernel(x_prompt, x_sample, ffn1_norm, ffn1_w_in, ffn1_w_out, mix_norm, w_in, b_gate, conv_w, pool_w, pool_scale, attn_q_norm, attn_k_norm, diff_q_norm, diff_k_norm, diff_lambda, diff_out_norm, w_branch, w_out, ffn2_norm, ffn2_w_in, ffn2_w_out):
    weights = (ffn1_norm, ffn1_w_in, ffn1_w_out, mix_norm, w_in, b_gate, conv_w, pool_w,
               pool_scale, attn_q_norm, attn_k_norm, diff_q_norm, diff_k_norm, diff_lambda,
               diff_out_norm, w_branch, w_out, ffn2_norm, ffn2_w_in, ffn2_w_out)
    layers = [_layer_weights(i, *weights) for i in range(DEPTH)]
    return (_trunk(x_prompt, layers), _trunk(x_sample, layers))
```

```python
import functools
import math

import jax
import jax.numpy as jnp
import numpy as np
from jax import lax
from jax.experimental import pallas as pl
from jax.experimental.pallas import tpu as pltpu

F32 = jnp.float32
BF16 = jnp.bfloat16

D_MODEL = 1024
DEPTH = 2
GRID_W = 64
HEAD_DIM = 64
D_FF = 2816
N_BRANCH = 4
BRANCH_W = 512
POOL_WINDOWS = (2, 4, 8, 16)
POOL_GROUP = BRANCH_W // 4
GQA_HEADS = 8
GQA_KV = 2
DIFF_HEADS = 4
DIFF_V = 2 * HEAD_DIM
AXIAL_THETA = 10000.0
ROPE_THETA = 500000.0
ROPE_DIM = HEAD_DIM // 4
EPS = 1e-6

_OFF_CONV = 0
_OFF_POOL = 3 * BRANCH_W
_OFF_CQ = _OFF_POOL + BRANCH_W
_OFF_CK = _OFF_CQ + GQA_HEADS * HEAD_DIM
_OFF_CV = _OFF_CK + GQA_KV * HEAD_DIM
_OFF_DQ = _OFF_CV + GQA_KV * HEAD_DIM
_OFF_DK = _OFF_DQ + DIFF_HEADS * 2 * HEAD_DIM
_OFF_DV = _OFF_DK + DIFF_HEADS * 2 * HEAD_DIM
_OFF_GATE = _OFF_DV + DIFF_HEADS * DIFF_V
IN_W = _OFF_GATE + N_BRANCH * D_MODEL

LANES = 128
BF16_SUBLANES = 16
VMEM_LIMIT_BYTES = 56 * 1024 * 1024

HALO = BF16_SUBLANES
ONES_ROWS = BF16_SUBLANES
NEG_BIG = -1e30
MXU_DEPTH = 256


def _tiles(seq):
    return dict(
        ffn_tm=512,
        proj_tm=256,
        merge_tm=512,
        gqa_tq=256,
        diff_tq=1024,
        tkv=min(1024, seq // 4),
    )


def _resident(shape):
    zeros = (0,) * len(shape)
    return pl.BlockSpec(shape, lambda *_: zeros, pipeline_mode=pl.Buffered(1))


def _params(n_axes):
    return pltpu.CompilerParams(dimension_semantics=("arbitrary",) * n_axes,
                                vmem_limit_bytes=VMEM_LIMIT_BYTES)


def _rms(xf, gain):
    return xf * lax.rsqrt(jnp.mean(xf * xf, axis=-1, keepdims=True) + EPS) * gain


def _ffn_kernel(x_ref, g_ref, win_ref, wout_ref, o_ref):
    x = x_ref[...]
    u = _rms(x, g_ref[...]).astype(BF16)
    h = jnp.dot(u, win_ref[...], preferred_element_type=F32)
    a = h[:, :D_FF]
    b = h[:, D_FF:]
    act = (a * jax.nn.sigmoid(a) * b).astype(BF16)
    y = jnp.dot(act, wout_ref[...], preferred_element_type=F32)
    o_ref[...] = x + 0.5 * y


def _ffn(x2, gain, w_in, w_out, tm):
    T = x2.shape[0]
    return pl.pallas_call(
        _ffn_kernel,
        grid=(T // tm,),
        in_specs=[pl.BlockSpec((tm, D_MODEL), lambda i: (i, 0)),
                  _resident((1, D_MODEL)),
                  _resident((D_MODEL, 2 * D_FF)),
                  _resident((D_FF, D_MODEL))],
        out_specs=pl.BlockSpec((tm, D_MODEL), lambda i: (i, 0)),
        out_shape=jax.ShapeDtypeStruct((T, D_MODEL), F32),
        compiler_params=_params(1),
        name="ffn",
    )(x2, gain, w_in, w_out)


def _head_norm_rope(z, bd, gain, cos, sin_a, sin_b, shift, scale):
    ss = jnp.dot((z * z).astype(BF16), bd, preferred_element_type=F32)
    zn = z * lax.rsqrt(ss * (1.0 / HEAD_DIM) + EPS) * gain
    out = zn * cos + pltpu.roll(zn, LANES - shift, 1) * sin_a + pltpu.roll(zn, shift, 1) * sin_b
    if scale != 1.0:
        out = out * scale
    return out.astype(BF16)


def _proj_kernel(x_ref, g_ref, w_ref, bgate_ref, bd_ref, qn_ref, kn_ref, dqn_ref, dkn_ref,
                 acos_ref, asa_ref, asb_ref, pcos_ref, psa_ref, psb_ref,
                 conv_o, pool_o, q_o, k_o, vt_o, dq_o, dk_o, dvt_o, gate_o):
    u = _rms(x_ref[...], g_ref[...]).astype(BF16)

    def proj(lo, width):
        return jnp.dot(u, w_ref[:, lo:lo + width], preferred_element_type=F32)

    conv_o[...] = proj(_OFF_CONV, 3 * BRANCH_W).astype(BF16)
    pool_o[...] = proj(_OFF_POOL, BRANCH_W).astype(BF16)

    bd = bd_ref[...]
    axial = (acos_ref[...], asa_ref[...], asb_ref[...], HEAD_DIM // 4)
    partial = (pcos_ref[...], psa_ref[...], psb_ref[...], ROPE_DIM // 2)
    q_scale = HEAD_DIM ** -0.5 * math.log2(math.e)

    def heads(lo, width, out_ref, gain_ref, tables, scale):
        z = proj(lo, width)
        gain = gain_ref[...]
        for j in range(width // LANES):
            sl = slice(j * LANES, (j + 1) * LANES)
            out_ref[:, sl] = _head_norm_rope(z[:, sl], bd, gain, *tables, scale)

    heads(_OFF_CQ, GQA_HEADS * HEAD_DIM, q_o, qn_ref, axial, q_scale)
    heads(_OFF_CK, GQA_KV * HEAD_DIM, k_o, kn_ref, axial, 1.0)
    def values_t(lo, n_heads, ch, out_ref):
        vt = proj(lo, n_heads * ch).T.astype(BF16)
        ones = jnp.ones((ONES_ROWS, vt.shape[1]), BF16)
        rows = ch + ONES_ROWS
        for h in range(n_heads):
            out_ref[h * rows:h * rows + ch, :] = vt[h * ch:(h + 1) * ch, :]
            out_ref[h * rows + ch:(h + 1) * rows, :] = ones

    values_t(_OFF_CV, GQA_KV, HEAD_DIM, vt_o)
    heads(_OFF_DQ, DIFF_HEADS * 2 * HEAD_DIM, dq_o, dqn_ref, partial, q_scale)
    heads(_OFF_DK, DIFF_HEADS * 2 * HEAD_DIM, dk_o, dkn_ref, partial, 1.0)
    values_t(_OFF_DV, DIFF_HEADS, DIFF_V, dvt_o)

    for n in range(N_BRANCH):
        sl = slice(n * D_MODEL, (n + 1) * D_MODEL)
        g = proj(_OFF_GATE + n * D_MODEL, D_MODEL) + bgate_ref[:, sl]
        gate_o[:, sl] = jax.nn.sigmoid(g).astype(BF16)


def _proj(x2, lw, tabs, seq, tm):
    T = x2.shape[0]
    n_s = seq // tm
    tok = lambda w: pl.BlockSpec((tm, w), lambda i: (i, 0))
    tab = pl.BlockSpec((tm, LANES), lambda i: (i % n_s, 0))
    tok_t = lambda w: pl.BlockSpec((w, tm), lambda i: (0, i))
    outs = ((3 * BRANCH_W, False), (BRANCH_W, False), (GQA_HEADS * HEAD_DIM, False),
            (GQA_KV * HEAD_DIM, False), (GQA_KV * (HEAD_DIM + ONES_ROWS), True),
            (DIFF_HEADS * 2 * HEAD_DIM, False), (DIFF_HEADS * 2 * HEAD_DIM, False),
            (DIFF_HEADS * (DIFF_V + ONES_ROWS), True), (N_BRANCH * D_MODEL, False))
    return pl.pallas_call(
        _proj_kernel,
        grid=(T // tm,),
        in_specs=[tok(D_MODEL), _resident((1, D_MODEL)), _resident((D_MODEL, IN_W)),
                  _resident((1, N_BRANCH * D_MODEL)), _resident((LANES, LANES)),
                  _resident((1, LANES)), _resident((1, LANES)), _resident((1, LANES)),
                  _resident((1, LANES))] + [tab] * 6,
        out_specs=[tok_t(w) if tr else tok(w) for w, tr in outs],
        out_shape=[jax.ShapeDtypeStruct((w, T) if tr else (T, w), BF16) for w, tr in outs],
        compiler_params=_params(1),
        name="proj",
    )(x2, lw["mix_norm"], lw["w_in"], lw["b_gate"], lw["bd"], lw["q_norm"], lw["k_norm"],
      lw["dq_norm"], lw["dk_norm"], *tabs)


def _flash_cols(qst_ref, k_ref, vt_ref, m_ref, acc_ref, s_ref, smax_ref, alpha_ref, p_ref,
                *, tkv, n_kv, pv_groups):
    m_ref[...] = jnp.full(m_ref.shape, NEG_BIG, F32)
    acc_ref[...] = jnp.zeros(acc_ref.shape, F32)

    ck = min(MXU_DEPTH, tkv)
    n_chunks = tkv // ck

    def kv_start(u, c):
        base = u * tkv if isinstance(u, int) else pl.multiple_of(u * tkv, tkv)
        return base + c * ck

    def slot(scores_of=None, softmax=False, pv_of=None):
        if pv_of is not None:
            alpha = alpha_ref[...]
        if softmax:
            m_prev = m_ref[...]
            m_new = jnp.maximum(m_prev, smax_ref[...])
            alpha_ref[...] = jnp.exp2(m_prev - m_new)
            m_ref[...] = m_new
        col_max = None
        for c in range(n_chunks):
            rows = slice(c * ck, (c + 1) * ck)
            if pv_of is not None:
                kv = pl.ds(kv_start(pv_of, c), ck)
                for row0, cols in pv_groups:
                    vt = vt_ref[row0:row0 + acc_ref.shape[0], kv]
                    part = jnp.dot(vt, p_ref[rows, cols], preferred_element_type=F32)
                    if c == 0:
                        acc_ref[:, cols] = alpha[:, cols] * acc_ref[:, cols] + part
                    else:
                        acc_ref[:, cols] += part
            if softmax:
                p_ref[rows, :] = jnp.exp2(s_ref[rows, :] - m_new).astype(BF16)
            if scores_of is not None:
                k = k_ref[0, pl.ds(kv_start(scores_of, c), ck), :]
                s = jnp.dot(k, qst_ref[...], preferred_element_type=F32)
                s_ref[rows, :] = s
                part = jnp.max(s, axis=0, keepdims=True)
                col_max = part if col_max is None else jnp.maximum(col_max, part)
        if scores_of is not None:
            smax_ref[...] = col_max

    slot(scores_of=0)
    slot(scores_of=1, softmax=True)

    def steady(u, carry):
        slot(scores_of=u, softmax=True, pv_of=u - 2)
        return carry

    lax.fori_loop(2, n_kv, steady, 0, unroll=2 if n_kv - 2 >= 4 else 1)
    slot(softmax=True, pv_of=n_kv - 2)
    slot(pv_of=n_kv - 1)


def _stack_queries(x, qst_ref, lo_cols, hi_cols):
    xt = x.astype(F32).T
    top = lax.broadcasted_iota(jnp.int32, xt.shape, 0) < HEAD_DIM
    zero = jnp.zeros_like(xt)
    qst_ref[:, lo_cols] = jnp.where(top, xt, zero).astype(BF16)
    qst_ref[:, hi_cols] = jnp.where(top, zero, xt).astype(BF16)


def _attn_scratch(R, tkv, acc_rows):
    return [pltpu.VMEM((LANES, R), BF16), pltpu.VMEM((1, R), F32),
            pltpu.VMEM((acc_rows, R), F32), pltpu.VMEM((tkv, R), F32),
            pltpu.VMEM((1, R), F32), pltpu.VMEM((1, R), F32),
            pltpu.VMEM((tkv, R), BF16)]


def _gqa_kernel(q_ref, k_ref, vt_ref, o_ref, qst_ref, m_ref, acc_ref, *pipe, tq, tkv, n_kv):
    n_blk = GQA_HEADS // GQA_KV
    col = lambda h: slice(h * tq, (h + 1) * tq)
    for j in range(n_blk):
        _stack_queries(q_ref[0, :, j * LANES:(j + 1) * LANES], qst_ref, col(j), col(n_blk + j))
    half = n_blk * tq
    _flash_cols(qst_ref, k_ref, vt_ref, m_ref, acc_ref, *pipe, tkv=tkv, n_kv=n_kv,
                pv_groups=((0, slice(0, half)), (HEAD_DIM + ONES_ROWS, slice(half, 2 * half))))
    out, den = slice(0, HEAD_DIM), slice(HEAD_DIM, HEAD_DIM + 1)
    for j in range(n_blk):
        c0, c1 = col(j), col(n_blk + j)
        top = acc_ref[out, c0] / acc_ref[den, c0]
        bot = acc_ref[out, c1] / acc_ref[den, c1]
        o_ref[0, :, j * LANES:(j + 1) * LANES] = (
            jnp.concatenate([top, bot], axis=0).T.astype(BF16))


def _gqa(q, k, vt, t):
    B, S, W = q.shape
    tq, tkv = t["gqa_tq"], t["tkv"]
    kern = functools.partial(_gqa_kernel, tq=tq, tkv=tkv, n_kv=S // tkv)
    return pl.pallas_call(
        kern,
        grid=(B, S // tq),
        in_specs=[pl.BlockSpec((1, tq, W), lambda b, i: (b, i, 0)),
                  pl.BlockSpec((1, S, LANES), lambda b, i: (b, 0, 0)),
                  pl.BlockSpec((GQA_KV * (HEAD_DIM + ONES_ROWS), S), lambda b, i: (0, b))],
        out_specs=pl.BlockSpec((1, tq, W), lambda b, i: (b, i, 0)),
        out_shape=jax.ShapeDtypeStruct((B, S, W), BF16),
        scratch_shapes=_attn_scratch(GQA_HEADS * tq, tkv, HEAD_DIM + ONES_ROWS),
        compiler_params=_params(2),
        name="gqa",
    )(q, k, vt)


def _diff_kernel(lam_ref, on_ref, q_ref, k_ref, vt_ref, o_ref, qst_ref, m_ref, acc_ref,
                 *pipe, tq, tkv, n_kv, lambda_init):
    c0, c1 = slice(0, tq), slice(tq, 2 * tq)
    _stack_queries(q_ref[0], qst_ref, c0, c1)
    _flash_cols(qst_ref, k_ref, vt_ref, m_ref, acc_ref, *pipe, tkv=tkv, n_kv=n_kv,
                pv_groups=((0, slice(0, 2 * tq)),))
    lv = lam_ref[...]
    lam = (jnp.exp(jnp.sum(lv[0:1] * lv[1:2], axis=1, keepdims=True))
           - jnp.exp(jnp.sum(lv[2:3] * lv[3:4], axis=1, keepdims=True)) + lambda_init)
    out, den = slice(0, DIFF_V), slice(DIFF_V, DIFF_V + 1)
    ot = (acc_ref[out, c0] / acc_ref[den, c0]
          - lam * (acc_ref[out, c1] / acc_ref[den, c1]))
    o_ref[0] = (_rms(ot.T, on_ref[...]) * (1.0 - lambda_init)).astype(BF16)


def _diff(q, k, vt, lam, out_norm, lambda_init, t):
    B, S, W = q.shape
    tq, tkv = t["diff_tq"], t["tkv"]
    kern = functools.partial(_diff_kernel, tq=tq, tkv=tkv, n_kv=S // tkv,
                             lambda_init=lambda_init)
    return pl.pallas_call(
        kern,
        grid=(B, DIFF_HEADS, S // tq),
        in_specs=[pl.BlockSpec((4, HEAD_DIM), lambda b, h, i: (0, 0)),
                  pl.BlockSpec((1, DIFF_V), lambda b, h, i: (0, 0)),
                  pl.BlockSpec((1, tq, LANES), lambda b, h, i: (b, i, h)),
                  pl.BlockSpec((1, S, LANES), lambda b, h, i: (b, 0, h)),
                  pl.BlockSpec((DIFF_V + ONES_ROWS, S), lambda b, h, i: (h, b))],
        out_specs=pl.BlockSpec((1, tq, LANES), lambda b, h, i: (b, i, h)),
        out_shape=jax.ShapeDtypeStruct((B, S, W), BF16),
        scratch_shapes=_attn_scratch(2 * tq, tkv, DIFF_V + ONES_ROWS),
        compiler_params=_params(3),
        name="diff",
    )(lam, out_norm, q, k, vt)


def _merge_kernel(x_ref, conv_ref, convp_ref, convn_ref, pool_ref, poolp_ref, pooln_ref,
                  og_ref, od_ref, gate_ref, cw_ref, pw_ref, ps_ref, wb_ref, wo_ref, o_ref,
                  cbuf, pbuf, *, tm, n_s, seq):
    si = pl.program_id(0) % n_s
    keep_prev = (si > 0).astype(F32)
    keep_next = (si < n_s - 1).astype(F32)
    W = BRANCH_W

    def gated(ref):
        blk = ref[...].astype(F32)
        return blk[:, 2 * W:3 * W] * blk[:, 0:W]
    cbuf[0:HALO, :] = gated(convp_ref) * keep_prev
    cbuf[HALO:HALO + tm, :] = gated(conv_ref)
    cbuf[HALO + tm:2 * HALO + tm, :] = gated(convn_ref) * keep_next
    cw = cw_ref[...]
    conv = (cw[0:1] * cbuf[HALO - 1:HALO - 1 + tm, :] + cw[1:2] * cbuf[HALO:HALO + tm, :]
            + cw[2:3] * cbuf[HALO + 1:HALO + 1 + tm, :])
    br_a = (conv_ref[:, W:2 * W].astype(F32) * conv).astype(BF16)

    pbuf[0:HALO, :] = poolp_ref[...].astype(F32) * keep_prev
    pbuf[HALO:HALO + tm, :] = pool_ref[...].astype(F32)
    pbuf[HALO + tm:2 * HALO + tm, :] = pooln_ref[...].astype(F32) * keep_next
    pos = si * tm + lax.broadcasted_iota(jnp.int32, (tm, 1), 0)
    pooled = []
    for gi, win in enumerate(POOL_WINDOWS):
        lanes = slice(gi * POOL_GROUP, (gi + 1) * POOL_GROUP)
        half = win // 2
        tot = pbuf[HALO - half:HALO - half + tm, lanes]
        for off in range(-half + 1, half):
            tot = tot + pbuf[HALO + off:HALO + off + tm, lanes]
        cnt = (jnp.clip(pos - half + win, 0, seq) - jnp.clip(pos - half, 0, seq)).astype(F32)
        mean_diff = (tot / cnt - pbuf[HALO:HALO + tm, lanes]).astype(BF16)
        pooled.append(jnp.dot(mean_diff, pw_ref[gi], preferred_element_type=F32))
    br_b = (jnp.concatenate(pooled, axis=1) * ps_ref[...]).astype(BF16)

    merged = None
    for n, br in enumerate((br_a, br_b, og_ref[...], od_ref[...])):
        y = jnp.dot(br, wb_ref[n], preferred_element_type=F32)
        y = gate_ref[:, n * D_MODEL:(n + 1) * D_MODEL].astype(F32) * y
        merged = y if merged is None else merged + y
    o_ref[...] = x_ref[...] + jnp.dot(merged.astype(BF16), wo_ref[...],
                                      preferred_element_type=F32)


def _merge(x2, conv, pool, og, od, gates, lw, seq, tm):
    T = x2.shape[0]
    n_s = seq // tm
    per = tm // HALO
    last = T // HALO - 1
    tok = lambda w: pl.BlockSpec((tm, w), lambda i: (i, 0))
    prev = lambda w: pl.BlockSpec((HALO, w), lambda i: (jnp.maximum(i * per - 1, 0), 0))
    nxt = lambda w: pl.BlockSpec((HALO, w), lambda i: (jnp.minimum((i + 1) * per, last), 0))
    kern = functools.partial(_merge_kernel, tm=tm, n_s=n_s, seq=seq)
    return pl.pallas_call(
        kern,
        grid=(T // tm,),
        in_specs=[tok(D_MODEL),
                  tok(3 * BRANCH_W), prev(3 * BRANCH_W), nxt(3 * BRANCH_W),
                  tok(BRANCH_W), prev(BRANCH_W), nxt(BRANCH_W),
                  tok(BRANCH_W), tok(BRANCH_W), tok(N_BRANCH * D_MODEL),
                  _resident((3, BRANCH_W)), _resident((4, POOL_GROUP, POOL_GROUP)),
                  _resident((1, BRANCH_W)), _resident((N_BRANCH, BRANCH_W, D_MODEL)),
                  _resident((D_MODEL, D_MODEL))],
        out_specs=tok(D_MODEL),
        out_shape=jax.ShapeDtypeStruct((T, D_MODEL), F32),
        scratch_shapes=[pltpu.VMEM((tm + 2 * HALO, BRANCH_W), F32),
                        pltpu.VMEM((tm + 2 * HALO, BRANCH_W), F32)],
        compiler_params=_params(1),
        name="merge",
    )(x2, conv, conv, conv, pool, pool, pool, og, od, gates,
      lw["conv_w"], lw["pool_w"], lw["pool_scale"], lw["w_branch"], lw["w_out"])


def _gqa_lane_perm():
    n_blk = GQA_HEADS // GQA_KV
    perm = np.empty(GQA_HEADS * HEAD_DIM, np.int32)
    for j in range(n_blk):
        for g in range(GQA_KV):
            dst = j * LANES + g * HEAD_DIM
            src = (g * n_blk + j) * HEAD_DIM
            perm[dst:dst + HEAD_DIM] = np.arange(src, src + HEAD_DIM)
    return perm


def _rope_tables(seq):
    def freqs(n, theta):
        half = n // 2
        return jnp.exp(-math.log(theta) * jnp.arange(half, dtype=F32) * (2.0 / n))

    def angles(pos, n, theta):
        return pos.astype(F32)[:, None] * freqs(n, theta)[None, :]

    def rotary(ang):
        c, s, z = jnp.cos(ang), jnp.sin(ang), jnp.zeros_like(ang)
        return (jnp.concatenate([c, c], 1), jnp.concatenate([-s, z], 1),
                jnp.concatenate([z, s], 1))

    rows = seq // GRID_W
    row = jnp.repeat(jnp.arange(rows), GRID_W)
    col = jnp.tile(jnp.arange(GRID_W), rows)
    half_dim = HEAD_DIM // 2
    ax = [jnp.concatenate([a, b], 1) for a, b in zip(rotary(angles(row, half_dim, AXIAL_THETA)),
                                                     rotary(angles(col, half_dim, AXIAL_THETA)))]
    rest = HEAD_DIM - ROPE_DIM
    pr = rotary(angles(jnp.arange(seq), ROPE_DIM, ROPE_THETA))
    pt = [jnp.concatenate([pr[0], jnp.ones((seq, rest), F32)], 1),
          jnp.concatenate([pr[1], jnp.zeros((seq, rest), F32)], 1),
          jnp.concatenate([pr[2], jnp.zeros((seq, rest), F32)], 1)]
    two = lambda a: jnp.concatenate([a, a], 1)
    return tuple(two(a) for a in ax) + tuple(two(a) for a in pt)


def _layer_weights(i, ffn1_norm, ffn1_w_in, ffn1_w_out, mix_norm, w_in, b_gate, conv_w, pool_w,
                   pool_scale, attn_q_norm, attn_k_norm, diff_q_norm, diff_k_norm, diff_lambda,
                   diff_out_norm, w_branch, w_out, ffn2_norm, ffn2_w_in, ffn2_w_out):
    perm = _gqa_lane_perm()
    wi = w_in[i]
    wi = wi.at[:, _OFF_CQ:_OFF_CK].set(wi[:, _OFF_CQ:_OFF_CK][:, perm])
    wb = w_branch[i]
    wb = wb.at[2].set(wb[2][perm, :])
    two = lambda g: jnp.concatenate([g, g]).reshape(1, LANES).astype(F32)
    blk = np.arange(LANES) // HEAD_DIM
    return dict(
        ffn1=(ffn1_norm[i].reshape(1, D_MODEL), ffn1_w_in[i].astype(BF16),
              ffn1_w_out[i].astype(BF16)),
        ffn2=(ffn2_norm[i].reshape(1, D_MODEL), ffn2_w_in[i].astype(BF16),
              ffn2_w_out[i].astype(BF16)),
        mix_norm=mix_norm[i].reshape(1, D_MODEL),
        w_in=wi.astype(BF16),
        b_gate=b_gate[i].reshape(1, N_BRANCH * D_MODEL),
        bd=jnp.asarray(blk[:, None] == blk[None, :], BF16),
        q_norm=two(attn_q_norm[i]), k_norm=two(attn_k_norm[i]),
        dq_norm=two(diff_q_norm[i]), dk_norm=two(diff_k_norm[i]),
        diff_lambda=diff_lambda[i].astype(F32),
        diff_out_norm=diff_out_norm[i].reshape(1, DIFF_V),
        lambda_init=0.8 - 0.6 * math.exp(-0.3 * i),
        conv_w=conv_w[i], pool_w=pool_w[i].astype(BF16),
        pool_scale=pool_scale[i].reshape(1, BRANCH_W),
        w_branch=wb.astype(BF16), w_out=w_out[i].astype(BF16),
    )


def _trunk(x, layers):
    B, S, _ = x.shape
    T = B * S
    t = _tiles(S)
    tabs = _rope_tables(S)
    x2 = x.reshape(T, D_MODEL)
    for lw in layers:
        x2 = _ffn(x2, *lw["ffn1"], t["ffn_tm"])
        conv, pool, q, k, vt, dq, dk, dvt, gates = _proj(x2, lw, tabs, S, t["proj_tm"])
        seq3 = lambda a: a.reshape(B, S, a.shape[-1])
        og = _gqa(seq3(q), seq3(k), vt, t)
        od = _diff(seq3(dq), seq3(dk), dvt, lw["diff_lambda"], lw["diff_out_norm"],
                   lw["lambda_init"], t)
        x2 = _merge(x2, conv, pool, og.reshape(T, -1), od.reshape(T, -1), gates, lw, S,
                    t["merge_tm"])
        x2 = _ffn(x2, *lw["ffn2"], t["ffn_tm"])
    return x2.reshape(B, S, D_MODEL)


def kernel(x_prompt, x_sample, ffn1_norm, ffn1_w_in, ffn1_w_out, mix_norm, w_in, b_gate, conv_w, pool_w, pool_scale, attn_q_norm, attn_k_norm, diff_q_norm, diff_k_norm, diff_lambda, diff_out_norm, w_branch, w_out, ffn2_norm, ffn2_w_in, ffn2_w_out):
    weights = (ffn1_norm, ffn1_w_in, ffn1_w_out, mix_norm, w_in, b_gate, conv_w, pool_w,
               pool_scale, attn_q_norm, attn_k_norm, diff_q_norm, diff_k_norm, diff_lambda,
               diff_out_norm, w_branch, w_out, ffn2_norm, ffn2_w_in, ffn2_w_out)
    layers = [_layer_weights(i, *weights) for i in range(DEPTH)]
    return (_trunk(x_prompt, layers), _trunk(x_sample, layers))
```

```python
import functools
import math

import jax
import jax.numpy as jnp
import numpy as np
from jax import lax
from jax.experimental import pallas as pl
from jax.experimental.pallas import tpu as pltpu

F32 = jnp.float32
BF16 = jnp.bfloat16

D_MODEL = 1024
DEPTH = 2
GRID_W = 64
HEAD_DIM = 64
D_FF = 2816
N_BRANCH = 4
BRANCH_W = 512
POOL_WINDOWS = (2, 4, 8, 16)
POOL_GROUP = BRANCH_W // 4
GQA_HEADS = 8
GQA_KV = 2
DIFF_HEADS = 4
DIFF_V = 2 * HEAD_DIM
AXIAL_THETA = 10000.0
ROPE_THETA = 500000.0
ROPE_DIM = HEAD_DIM // 4
EPS = 1e-6

_OFF_CONV = 0
_OFF_POOL = 3 * BRANCH_W
_OFF_CQ = _OFF_POOL + BRANCH_W
_OFF_CK = _OFF_CQ + GQA_HEADS * HEAD_DIM
_OFF_CV = _OFF_CK + GQA_KV * HEAD_DIM
_OFF_DQ = _OFF_CV + GQA_KV * HEAD_DIM
_OFF_DK = _OFF_DQ + DIFF_HEADS * 2 * HEAD_DIM
_OFF_DV = _OFF_DK + DIFF_HEADS * 2 * HEAD_DIM
_OFF_GATE = _OFF_DV + DIFF_HEADS * DIFF_V
IN_W = _OFF_GATE + N_BRANCH * D_MODEL

LANES = 128
BF16_SUBLANES = 16
VMEM_LIMIT_BYTES = 56 * 1024 * 1024

HALO = BF16_SUBLANES
ONES_ROWS = BF16_SUBLANES
NEG_BIG = -1e30
MXU_DEPTH = 256


def _tiles(seq):
    return dict(
        ffn_tm=512,
        proj_tm=256,
        merge_tm=512,
        gqa_tq=256,
        diff_tq=1024,
        tkv=min(1024, seq // 4),
    )


def _resident(shape):
    zeros = (0,) * len(shape)
    return pl.BlockSpec(shape, lambda *_: zeros, pipeline_mode=pl.Buffered(1))


def _params(n_axes):
    return pltpu.CompilerParams(dimension_semantics=("arbitrary",) * n_axes,
                                vmem_limit_bytes=VMEM_LIMIT_BYTES)


def _rms(xf, gain):
    return xf * lax.rsqrt(jnp.mean(xf * xf, axis=-1, keepdims=True) + EPS) * gain


def _ffn_kernel(x_ref, g_ref, win_ref, wout_ref, o_ref):
    x = x_ref[...]
    u = _rms(x, g_ref[...]).astype(BF16)
    h = jnp.dot(u, win_ref[...], preferred_element_type=F32)
    a = h[:, :D_FF]
    b = h[:, D_FF:]
    act = (a * jax.nn.sigmoid(a) * b).astype(BF16)
    y = jnp.dot(act, wout_ref[...], preferred_element_type=F32)
    o_ref[...] = x + 0.5 * y


def _ffn(x2, gain, w_in, w_out, tm):
    T = x2.shape[0]
    return pl.pallas_call(
        _ffn_kernel,
        grid=(T // tm,),
        in_specs=[pl.BlockSpec((tm, D_MODEL), lambda i: (i, 0)),
                  _resident((1, D_MODEL)),
                  _resident((D_MODEL, 2 * D_FF)),
                  _resident((D_FF, D_MODEL))],
        out_specs=pl.BlockSpec((tm, D_MODEL), lambda i: (i, 0)),
        out_shape=jax.ShapeDtypeStruct((T, D_MODEL), F32),
        compiler_params=_params(1),
        name="ffn",
    )(x2, gain, w_in, w_out)


def _head_norm_rope(z, bd, gain, cos, sin_a, sin_b, shift, scale):
    ss = jnp.dot((z * z).astype(BF16), bd, preferred_element_type=F32)
    zn = z * lax.rsqrt(ss * (1.0 / HEAD_DIM) + EPS) * gain
    out = zn * cos + pltpu.roll(zn, LANES - shift, 1) * sin_a + pltpu.roll(zn, shift, 1) * sin_b
    if scale != 1.0:
        out = out * scale
    return out.astype(BF16)


def _proj_kernel(x_ref, g_ref, w_ref, bgate_ref, bd_ref, qn_ref, kn_ref, dqn_ref, dkn_ref,
                 acos_ref, asa_ref, asb_ref, pcos_ref, psa_ref, psb_ref,
                 conv_o, pool_o, q_o, k_o, vt_o, dq_o, dk_o, dvt_o, gate_o):
    u = _rms(x_ref[...], g_ref[...]).astype(BF16)

    def proj(lo, width):
        return jnp.dot(u, w_ref[:, lo:lo + width], preferred_element_type=F32)

    conv_o[...] = proj(_OFF_CONV, 3 * BRANCH_W).astype(BF16)
    pool_o[...] = proj(_OFF_POOL, BRANCH_W).astype(BF16)

    bd = bd_ref[...]
    axial = (acos_ref[...], asa_ref[...], asb_ref[...], HEAD_DIM // 4)
    partial = (pcos_ref[...], psa_ref[...], psb_ref[...], ROPE_DIM // 2)
    q_scale = HEAD_DIM ** -0.5 * math.log2(math.e)

    def heads(lo, width, out_ref, gain_ref, tables, scale):
        z = proj(lo, width)
        gain = gain_ref[...]
        for j in range(width // LANES):
            sl = slice(j * LANES, (j + 1) * LANES)
            out_ref[:, sl] = _head_norm_rope(z[:, sl], bd, gain, *tables, scale)

    heads(_OFF_CQ, GQA_HEADS * HEAD_DIM, q_o, qn_ref, axial, q_scale)
    heads(_OFF_CK, GQA_KV * HEAD_DIM, k_o, kn_ref, axial, 1.0)
    def values_t(lo, n_heads, ch, out_ref):
        vt = proj(lo, n_heads * ch).T.astype(BF16)
        ones = jnp.ones((ONES_ROWS, vt.shape[1]), BF16)
        rows = ch + ONES_ROWS
        for h in range(n_heads):
            out_ref[h * rows:h * rows + ch, :] = vt[h * ch:(h + 1) * ch, :]
            out_ref[h * rows + ch:(h + 1) * rows, :] = ones

    values_t(_OFF_CV, GQA_KV, HEAD_DIM, vt_o)
    heads(_OFF_DQ, DIFF_HEADS * 2 * HEAD_DIM, dq_o, dqn_ref, partial, q_scale)
    heads(_OFF_DK, DIFF_HEADS * 2 * HEAD_DIM, dk_o, dkn_ref, partial, 1.0)
    values_t(_OFF_DV, DIFF_HEADS, DIFF_V, dvt_o)

    for n in range(N_BRANCH):
        sl = slice(n * D_MODEL, (n + 1) * D_MODEL)
        g = proj(_OFF_GATE + n * D_MODEL, D_MODEL) + bgate_ref[:, sl]
        gate_o[:, sl] = jax.nn.sigmoid(g).astype(BF16)


def _proj(x2, lw, tabs, seq, tm):
    T = x2.shape[0]
    n_s = seq // tm
    tok = lambda w: pl.BlockSpec((tm, w), lambda i: (i, 0))
    tab = pl.BlockSpec((tm, LANES), lambda i: (i % n_s, 0))
    tok_t = lambda w: pl.BlockSpec((w, tm), lambda i: (0, i))
    outs = ((3 * BRANCH_W, False), (BRANCH_W, False), (GQA_HEADS * HEAD_DIM, False),
            (GQA_KV * HEAD_DIM, False), (GQA_KV * (HEAD_DIM + ONES_ROWS), True),
            (DIFF_HEADS * 2 * HEAD_DIM, False), (DIFF_HEADS * 2 * HEAD_DIM, False),
            (DIFF_HEADS * (DIFF_V + ONES_ROWS), True), (N_BRANCH * D_MODEL, False))
    return pl.pallas_call(
        _proj_kernel,
        grid=(T // tm,),
        in_specs=[tok(D_MODEL), _resident((1, D_MODEL)), _resident((D_MODEL, IN_W)),
                  _resident((1, N_BRANCH * D_MODEL)), _resident((LANES, LANES)),
                  _resident((1, LANES)), _resident((1, LANES)), _resident((1, LANES)),
                  _resident((1, LANES))] + [tab] * 6,
        out_specs=[tok_t(w) if tr else tok(w) for w, tr in outs],
        out_shape=[jax.ShapeDtypeStruct((w, T) if tr else (T, w), BF16) for w, tr in outs],
        compiler_params=_params(1),
        name="proj",
    )(x2, lw["mix_norm"], lw["w_in"], lw["b_gate"], lw["bd"], lw["q_norm"], lw["k_norm"],
      lw["dq_norm"], lw["dk_norm"], *tabs)


def _flash_cols(qst_ref, k_ref, vt_ref, m_ref, acc_ref, s_ref, smax_ref, alpha_ref, p_ref,
                *, tkv, n_kv, pv_groups):
    m_ref[...] = jnp.full(m_ref.shape, NEG_BIG, F32)
    acc_ref[...] = jnp.zeros(acc_ref.shape, F32)

    R = m_ref.shape[1]
    ck = min(MXU_DEPTH, tkv)
    n_chunks = tkv // ck

    def kv_start(u, c):
        base = u * tkv if isinstance(u, int) else pl.multiple_of(u * tkv, tkv)
        return base + c * ck

    def slot(scores_of=None, softmax=False, pv_of=None):
        if pv_of is not None:
            alpha = alpha_ref[...]
        if softmax:
            m_prev = m_ref[...]
            m_new = jnp.maximum(m_prev, smax_ref[...])
            alpha_ref[...] = jnp.exp2(m_prev - m_new)
            m_ref[...] = m_new
        col_max = None
        for c in range(n_chunks):
            rows = slice(c * ck, (c + 1) * ck)
            if pv_of is not None:
                kv = pl.ds(kv_start(pv_of, c), ck)
                for row0, cols in pv_groups:
                    vt = vt_ref[row0:row0 + acc_ref.shape[0], kv]
                    part = jnp.dot(vt, p_ref[rows, cols], preferred_element_type=F32)
                    if c == 0:
                        acc_ref[:, cols] = alpha[:, cols] * acc_ref[:, cols] + part
                    else:
                        acc_ref[:, cols] += part
            if softmax:
                p_ref[rows, :] = jnp.exp2(s_ref[rows, :R] - m_new).astype(BF16)
            if scores_of is not None:
                k = k_ref[0, pl.ds(kv_start(scores_of, c), ck), :]
                s = jnp.dot(k, qst_ref[...], preferred_element_type=F32)
                s_ref[rows, :R] = s
                part = jnp.max(s, axis=0, keepdims=True)
                col_max = part if col_max is None else jnp.maximum(col_max, part)
        if scores_of is not None:
            smax_ref[...] = col_max

    slot(scores_of=0)
    slot(scores_of=1, softmax=True)

    def steady(u, carry):
        slot(scores_of=u, softmax=True, pv_of=u - 2)
        return carry

    lax.fori_loop(2, n_kv, steady, 0, unroll=2 if n_kv - 2 >= 4 else 1)
    slot(softmax=True, pv_of=n_kv - 2)
    slot(pv_of=n_kv - 1)


def _stack_queries(x, qst_ref, lo_cols, hi_cols):
    xt = x.astype(F32).T
    top = lax.broadcasted_iota(jnp.int32, xt.shape, 0) < HEAD_DIM
    zero = jnp.zeros_like(xt)
    qst_ref[:, lo_cols] = jnp.where(top, xt, zero).astype(BF16)
    qst_ref[:, hi_cols] = jnp.where(top, zero, xt).astype(BF16)


def _attn_scratch(R, tkv, acc_rows):
    return [pltpu.VMEM((LANES, R), BF16), pltpu.VMEM((1, R), F32),
            pltpu.VMEM((acc_rows, R + LANES), F32), pltpu.VMEM((tkv, R + LANES), F32),
            pltpu.VMEM((1, R), F32), pltpu.VMEM((1, R), F32),
            pltpu.VMEM((tkv, R), BF16)]


def _gqa_kernel(q_ref, k_ref, vt_ref, o_ref, qst_ref, m_ref, acc_ref, *pipe, tq, tkv, n_kv):
    n_blk = GQA_HEADS // GQA_KV
    col = lambda h: slice(h * tq, (h + 1) * tq)
    for j in range(n_blk):
        _stack_queries(q_ref[0, :, j * LANES:(j + 1) * LANES], qst_ref, col(j), col(n_blk + j))
    half = n_blk * tq
    _flash_cols(qst_ref, k_ref, vt_ref, m_ref, acc_ref, *pipe, tkv=tkv, n_kv=n_kv,
                pv_groups=((0, slice(0, half)), (HEAD_DIM + ONES_ROWS, slice(half, 2 * half))))
    out, den = slice(0, HEAD_DIM), slice(HEAD_DIM, HEAD_DIM + 1)
    for j in range(n_blk):
        c0, c1 = col(j), col(n_blk + j)
        top = acc_ref[out, c0] / acc_ref[den, c0]
        bot = acc_ref[out, c1] / acc_ref[den, c1]
        o_ref[0, :, j * LANES:(j + 1) * LANES] = (
            jnp.concatenate([top, bot], axis=0).T.astype(BF16))


def _gqa(q, k, vt, t):
    B, S, W = q.shape
    tq, tkv = t["gqa_tq"], t["tkv"]
    kern = functools.partial(_gqa_kernel, tq=tq, tkv=tkv, n_kv=S // tkv)
    return pl.pallas_call(
        kern,
        grid=(B, S // tq),
        in_specs=[pl.BlockSpec((1, tq, W), lambda b, i: (b, i, 0)),
                  pl.BlockSpec((1, S, LANES), lambda b, i: (b, 0, 0)),
                  pl.BlockSpec((GQA_KV * (HEAD_DIM + ONES_ROWS), S), lambda b, i: (0, b))],
        out_specs=pl.BlockSpec((1, tq, W), lambda b, i: (b, i, 0)),
        out_shape=jax.ShapeDtypeStruct((B, S, W), BF16),
        scratch_shapes=_attn_scratch(GQA_HEADS * tq, tkv, HEAD_DIM + ONES_ROWS),
        compiler_params=_params(2),
        name="gqa",
    )(q, k, vt)


def _diff_kernel(lam_ref, on_ref, q_ref, k_ref, vt_ref, o_ref, qst_ref, m_ref, acc_ref,
                 *pipe, tq, tkv, n_kv, lambda_init):
    c0, c1 = slice(0, tq), slice(tq, 2 * tq)
    _stack_queries(q_ref[0], qst_ref, c0, c1)
    _flash_cols(qst_ref, k_ref, vt_ref, m_ref, acc_ref, *pipe, tkv=tkv, n_kv=n_kv,
                pv_groups=((0, slice(0, 2 * tq)),))
    lv = lam_ref[...]
    lam = (jnp.exp(jnp.sum(lv[0:1] * lv[1:2], axis=1, keepdims=True))
           - jnp.exp(jnp.sum(lv[2:3] * lv[3:4], axis=1, keepdims=True)) + lambda_init)
    out, den = slice(0, DIFF_V), slice(DIFF_V, DIFF_V + 1)
    ot = (acc_ref[out, c0] / acc_ref[den, c0]
          - lam * (acc_ref[out, c1] / acc_ref[den, c1]))
    o_ref[0] = (_rms(ot.T, on_ref[...]) * (1.0 - lambda_init)).astype(BF16)


def _diff(q, k, vt, lam, out_norm, lambda_init, t):
    B, S, W = q.shape
    tq, tkv = t["diff_tq"], t["tkv"]
    kern = functools.partial(_diff_kernel, tq=tq, tkv=tkv, n_kv=S // tkv,
                             lambda_init=lambda_init)
    return pl.pallas_call(
        kern,
        grid=(B, DIFF_HEADS, S // tq),
        in_specs=[pl.BlockSpec((4, HEAD_DIM), lambda b, h, i: (0, 0)),
                  pl.BlockSpec((1, DIFF_V), lambda b, h, i: (0, 0)),
                  pl.BlockSpec((1, tq, LANES), lambda b, h, i: (b, i, h)),
                  pl.BlockSpec((1, S, LANES), lambda b, h, i: (b, 0, h)),
                  pl.BlockSpec((DIFF_V + ONES_ROWS, S), lambda b, h, i: (h, b))],
        out_specs=pl.BlockSpec((1, tq, LANES), lambda b, h, i: (b, i, h)),
        out_shape=jax.ShapeDtypeStruct((B, S, W), BF16),
        scratch_shapes=_attn_scratch(2 * tq, tkv, DIFF_V + ONES_ROWS),
        compiler_params=_params(3),
        name="diff",
    )(lam, out_norm, q, k, vt)


def _merge_kernel(x_ref, conv_ref, convp_ref, convn_ref, pool_ref, poolp_ref, pooln_ref,
                  og_ref, od_ref, gate_ref, cw_ref, pw_ref, ps_ref, wb_ref, wo_ref, o_ref,
                  cbuf, pbuf, *, tm, n_s, seq):
    si = pl.program_id(0) % n_s
    keep_prev = (si > 0).astype(F32)
    keep_next = (si < n_s - 1).astype(F32)
    W = BRANCH_W

    def gated(ref):
        blk = ref[...].astype(F32)
        return blk[:, 2 * W:3 * W] * blk[:, 0:W]
    cbuf[0:HALO, :] = gated(convp_ref) * keep_prev
    cbuf[HALO:HALO + tm, :] = gated(conv_ref)
    cbuf[HALO + tm:2 * HALO + tm, :] = gated(convn_ref) * keep_next
    cw = cw_ref[...]
    conv = (cw[0:1] * cbuf[HALO - 1:HALO - 1 + tm, :] + cw[1:2] * cbuf[HALO:HALO + tm, :]
            + cw[2:3] * cbuf[HALO + 1:HALO + 1 + tm, :])
    br_a = (conv_ref[:, W:2 * W].astype(F32) * conv).astype(BF16)

    pbuf[0:HALO, :] = poolp_ref[...].astype(F32) * keep_prev
    pbuf[HALO:HALO + tm, :] = pool_ref[...].astype(F32)
    pbuf[HALO + tm:2 * HALO + tm, :] = pooln_ref[...].astype(F32) * keep_next
    pos = si * tm + lax.broadcasted_iota(jnp.int32, (tm, 1), 0)
    pooled = []
    for gi, win in enumerate(POOL_WINDOWS):
        lanes = slice(gi * POOL_GROUP, (gi + 1) * POOL_GROUP)
        half = win // 2
        tot = pbuf[HALO - half:HALO - half + tm, lanes]
        for off in range(-half + 1, half):
            tot = tot + pbuf[HALO + off:HALO + off + tm, lanes]
        cnt = (jnp.clip(pos - half + win, 0, seq) - jnp.clip(pos - half, 0, seq)).astype(F32)
        mean_diff = (tot / cnt - pbuf[HALO:HALO + tm, lanes]).astype(BF16)
        pooled.append(jnp.dot(mean_diff, pw_ref[gi], preferred_element_type=F32))
    br_b = (jnp.concatenate(pooled, axis=1) * ps_ref[...]).astype(BF16)

    merged = None
    for n, br in enumerate((br_a, br_b, og_ref[...], od_ref[...])):
        y = jnp.dot(br, wb_ref[n], preferred_element_type=F32)
        y = gate_ref[:, n * D_MODEL:(n + 1) * D_MODEL].astype(F32) * y
        merged = y if merged is None else merged + y
    o_ref[...] = x_ref[...] + jnp.dot(merged.astype(BF16), wo_ref[...],
                                      preferred_element_type=F32)


def _merge(x2, conv, pool, og, od, gates, lw, seq, tm):
    T = x2.shape[0]
    n_s = seq // tm
    per = tm // HALO
    last = T // HALO - 1
    tok = lambda w: pl.BlockSpec((tm, w), lambda i: (i, 0))
    prev = lambda w: pl.BlockSpec((HALO, w), lambda i: (jnp.maximum(i * per - 1, 0), 0))
    nxt = lambda w: pl.BlockSpec((HALO, w), lambda i: (jnp.minimum((i + 1) * per, last), 0))
    kern = functools.partial(_merge_kernel, tm=tm, n_s=n_s, seq=seq)
    return pl.pallas_call(
        kern,
        grid=(T // tm,),
        in_specs=[tok(D_MODEL),
                  tok(3 * BRANCH_W), prev(3 * BRANCH_W), nxt(3 * BRANCH_W),
                  tok(BRANCH_W), prev(BRANCH_W), nxt(BRANCH_W),
                  tok(BRANCH_W), tok(BRANCH_W), tok(N_BRANCH * D_MODEL),
                  _resident((3, BRANCH_W)), _resident((4, POOL_GROUP, POOL_GROUP)),
                  _resident((1, BRANCH_W)), _resident((N_BRANCH, BRANCH_W, D_MODEL)),
                  _resident((D_MODEL, D_MODEL))],
        out_specs=tok(D_MODEL),
        out_shape=jax.ShapeDtypeStruct((T, D_MODEL), F32),
        scratch_shapes=[pltpu.VMEM((tm + 2 * HALO, BRANCH_W), F32),
                        pltpu.VMEM((tm + 2 * HALO, BRANCH_W), F32)],
        compiler_params=_params(1),
        name="merge",
    )(x2, conv, conv, conv, pool, pool, pool, og, od, gates,
      lw["conv_w"], lw["pool_w"], lw["pool_scale"], lw["w_branch"], lw["w_out"])


def _gqa_lane_perm():
    n_blk = GQA_HEADS // GQA_KV
    perm = np.empty(GQA_HEADS * HEAD_DIM, np.int32)
    for j in range(n_blk):
        for g in range(GQA_KV):
            dst = j * LANES + g * HEAD_DIM
            src = (g * n_blk + j) * HEAD_DIM
            perm[dst:dst + HEAD_DIM] = np.arange(src, src + HEAD_DIM)
    return perm


def _rope_tables(seq):
    def freqs(n, theta):
        half = n // 2
        return jnp.exp(-math.log(theta) * jnp.arange(half, dtype=F32) * (2.0 / n))

    def angles(pos, n, theta):
        return pos.astype(F32)[:, None] * freqs(n, theta)[None, :]

    def rotary(ang):
        c, s, z = jnp.cos(ang), jnp.sin(ang), jnp.zeros_like(ang)
        return (jnp.concatenate([c, c], 1), jnp.concatenate([-s, z], 1),
                jnp.concatenate([z, s], 1))

    rows = seq // GRID_W
    row = jnp.repeat(jnp.arange(rows), GRID_W)
    col = jnp.tile(jnp.arange(GRID_W), rows)
    half_dim = HEAD_DIM // 2
    ax = [jnp.concatenate([a, b], 1) for a, b in zip(rotary(angles(row, half_dim, AXIAL_THETA)),
                                                     rotary(angles(col, half_dim, AXIAL_THETA)))]
    rest = HEAD_DIM - ROPE_DIM
    pr = rotary(angles(jnp.arange(seq), ROPE_DIM, ROPE_THETA))
    pt = [jnp.concatenate([pr[0], jnp.ones((seq, rest), F32)], 1),
          jnp.concatenate([pr[1], jnp.zeros((seq, rest), F32)], 1),
          jnp.concatenate([pr[2], jnp.zeros((seq, rest), F32)], 1)]
    two = lambda a: jnp.concatenate([a, a], 1)
    return tuple(two(a) for a in ax) + tuple(two(a) for a in pt)


def _layer_weights(i, ffn1_norm, ffn1_w_in, ffn1_w_out, mix_norm, w_in, b_gate, conv_w, pool_w,
                   pool_scale, attn_q_norm, attn_k_norm, diff_q_norm, diff_k_norm, diff_lambda,
                   diff_out_norm, w_branch, w_out, ffn2_norm, ffn2_w_in, ffn2_w_out):
    perm = _gqa_lane_perm()
    wi = w_in[i]
    wi = wi.at[:, _OFF_CQ:_OFF_CK].set(wi[:, _OFF_CQ:_OFF_CK][:, perm])
    wb = w_branch[i]
    wb = wb.at[2].set(wb[2][perm, :])
    two = lambda g: jnp.concatenate([g, g]).reshape(1, LANES).astype(F32)
    blk = np.arange(LANES) // HEAD_DIM
    return dict(
        ffn1=(ffn1_norm[i].reshape(1, D_MODEL), ffn1_w_in[i].astype(BF16),
              ffn1_w_out[i].astype(BF16)),
        ffn2=(ffn2_norm[i].reshape(1, D_MODEL), ffn2_w_in[i].astype(BF16),
              ffn2_w_out[i].astype(BF16)),
        mix_norm=mix_norm[i].reshape(1, D_MODEL),
        w_in=wi.astype(BF16),
        b_gate=b_gate[i].reshape(1, N_BRANCH * D_MODEL),
        bd=jnp.asarray(blk[:, None] == blk[None, :], BF16),
        q_norm=two(attn_q_norm[i]), k_norm=two(attn_k_norm[i]),
        dq_norm=two(diff_q_norm[i]), dk_norm=two(diff_k_norm[i]),
        diff_lambda=diff_lambda[i].astype(F32),
        diff_out_norm=diff_out_norm[i].reshape(1, DIFF_V),
        lambda_init=0.8 - 0.6 * math.exp(-0.3 * i),
        conv_w=conv_w[i], pool_w=pool_w[i].astype(BF16),
        pool_scale=pool_scale[i].reshape(1, BRANCH_W),
        w_branch=wb.astype(BF16), w_out=w_out[i].astype(BF16),
    )


def _trunk(x, layers):
    B, S, _ = x.shape
    T = B * S
    t = _tiles(S)
    tabs = _rope_tables(S)
    x2 = x.reshape(T, D_MODEL)
    for lw in layers:
        x2 = _ffn(x2, *lw["ffn1"], t["ffn_tm"])
        conv, pool, q, k, vt, dq, dk, dvt, gates = _proj(x2, lw, tabs, S, t["proj_tm"])
        seq3 = lambda a: a.reshape(B, S, a.shape[-1])
        og = _gqa(seq3(q), seq3(k), vt, t)
        od = _diff(seq3(dq), seq3(dk), dvt, lw["diff_lambda"], lw["diff_out_norm"],
                   lw["lambda_init"], t)
        x2 = _merge(x2, conv, pool, og.reshape(T, -1), od.reshape(T, -1), gates, lw, S,
                    t["merge_tm"])
        x2 = _ffn(x2, *lw["ffn2"], t["ffn_tm"])
    return x2.reshape(B, S, D_MODEL)


def kernel(x_prompt, x_sample, ffn1_norm, ffn1_w_in, ffn1_w_out, mix_norm, w_in, b_gate, conv_w, pool_w, pool_scale, attn_q_norm, attn_k_norm, diff_q_norm, diff_k_norm, diff_lambda, diff_out_norm, w_branch, w_out, ffn2_norm, ffn2_w_in, ffn2_w_out):
    weights = (ffn1_norm, ffn1_w_in, ffn1_w_out, mix_norm, w_in, b_gate, conv_w, pool_w,
               pool_scale, attn_q_norm, attn_k_norm, diff_q_norm, diff_k_norm, diff_lambda,
               diff_out_norm, w_branch, w_out, ffn2_norm, ffn2_w_in, ffn2_w_out)
    layers = [_layer_weights(i, *weights) for i in range(DEPTH)]
    return (_trunk(x_prompt, layers), _trunk(x_sample, layers))
```

```python
import functools
import math

import jax
import jax.numpy as jnp
import numpy as np
from jax import lax
from jax.experimental import pallas as pl
from jax.experimental.pallas import tpu as pltpu

F32 = jnp.float32
BF16 = jnp.bfloat16

D_MODEL = 1024
DEPTH = 2
GRID_W = 64
HEAD_DIM = 64
D_FF = 2816
N_BRANCH = 4
BRANCH_W = 512
POOL_WINDOWS = (2, 4, 8, 16)
POOL_GROUP = BRANCH_W // 4
GQA_HEADS = 8
GQA_KV = 2
DIFF_HEADS = 4
DIFF_V = 2 * HEAD_DIM
AXIAL_THETA = 10000.0
ROPE_THETA = 500000.0
ROPE_DIM = HEAD_DIM // 4
EPS = 1e-6

_OFF_CONV = 0
_OFF_POOL = 3 * BRANCH_W
_OFF_CQ = _OFF_POOL + BRANCH_W
_OFF_CK = _OFF_CQ + GQA_HEADS * HEAD_DIM
_OFF_CV = _OFF_CK + GQA_KV * HEAD_DIM
_OFF_DQ = _OFF_CV + GQA_KV * HEAD_DIM
_OFF_DK = _OFF_DQ + DIFF_HEADS * 2 * HEAD_DIM
_OFF_DV = _OFF_DK + DIFF_HEADS * 2 * HEAD_DIM
_OFF_GATE = _OFF_DV + DIFF_HEADS * DIFF_V
IN_W = _OFF_GATE + N_BRANCH * D_MODEL

LANES = 128
BF16_SUBLANES = 16
VMEM_LIMIT_BYTES = 56 * 1024 * 1024

HALO = BF16_SUBLANES
ONES_ROWS = BF16_SUBLANES
NEG_BIG = -1e30
MXU_DEPTH = 256
SCORE_TILE_ELEMS = 1024 * 2048


def _tiles(seq):
    tkv = min(1024, seq // 4)
    cols = SCORE_TILE_ELEMS // tkv
    return dict(
        ffn_tm=512,
        proj_tm=256,
        merge_tm=512,
        gqa_tq=min(cols // GQA_HEADS, seq),
        diff_tq=min(cols // 2, seq),
        tkv=tkv,
    )


def _resident(shape):
    zeros = (0,) * len(shape)
    return pl.BlockSpec(shape, lambda *_: zeros, pipeline_mode=pl.Buffered(1))


def _params(n_axes):
    return pltpu.CompilerParams(dimension_semantics=("arbitrary",) * n_axes,
                                vmem_limit_bytes=VMEM_LIMIT_BYTES)


def _rms(xf, gain):
    return xf * lax.rsqrt(jnp.mean(xf * xf, axis=-1, keepdims=True) + EPS) * gain


def _ffn_kernel(x_ref, g_ref, win_ref, wout_ref, o_ref):
    x = x_ref[...]
    u = _rms(x, g_ref[...]).astype(BF16)
    h = jnp.dot(u, win_ref[...], preferred_element_type=F32)
    a = h[:, :D_FF]
    b = h[:, D_FF:]
    act = (a * jax.nn.sigmoid(a) * b).astype(BF16)
    y = jnp.dot(act, wout_ref[...], preferred_element_type=F32)
    o_ref[...] = x + 0.5 * y


def _ffn(x2, gain, w_in, w_out, tm):
    T = x2.shape[0]
    return pl.pallas_call(
        _ffn_kernel,
        grid=(T // tm,),
        in_specs=[pl.BlockSpec((tm, D_MODEL), lambda i: (i, 0)),
                  _resident((1, D_MODEL)),
                  _resident((D_MODEL, 2 * D_FF)),
                  _resident((D_FF, D_MODEL))],
        out_specs=pl.BlockSpec((tm, D_MODEL), lambda i: (i, 0)),
        out_shape=jax.ShapeDtypeStruct((T, D_MODEL), F32),
        compiler_params=_params(1),
        name="ffn",
    )(x2, gain, w_in, w_out)


def _head_sum_squares(z, bd):
    return jnp.dot((z * z).astype(BF16), bd, preferred_element_type=F32)


def _head_norm_rope(z, ss, gain, cos, sin_a, sin_b, shift, scale):
    zn = z * lax.rsqrt(ss * (1.0 / HEAD_DIM) + EPS) * gain
    out = zn * cos + pltpu.roll(zn, LANES - shift, 1) * sin_a + pltpu.roll(zn, shift, 1) * sin_b
    if scale != 1.0:
        out = out * scale
    return out.astype(BF16)


def _proj_kernel(x_ref, g_ref, w_ref, bgate_ref, bd_ref, qn_ref, kn_ref, dqn_ref, dkn_ref,
                 acos_ref, asa_ref, asb_ref, pcos_ref, psa_ref, psb_ref,
                 conv_o, pool_o, q_o, k_o, vt_o, dq_o, dk_o, dvt_o, gate_o):
    u = _rms(x_ref[...], g_ref[...]).astype(BF16)

    def proj(lo, width):
        return jnp.dot(u, w_ref[:, lo:lo + width], preferred_element_type=F32)

    conv_o[...] = proj(_OFF_CONV, 3 * BRANCH_W).astype(BF16)
    pool_o[...] = proj(_OFF_POOL, BRANCH_W).astype(BF16)

    bd = bd_ref[...]
    axial = (acos_ref[...], asa_ref[...], asb_ref[...], HEAD_DIM // 4)
    partial = (pcos_ref[...], psa_ref[...], psb_ref[...], ROPE_DIM // 2)
    q_scale = HEAD_DIM ** -0.5 * math.log2(math.e)

    def heads(z, width, out_ref, gain_ref, tables, scale):
        gain = gain_ref[...]
        for j in range(0, width, 2 * LANES):
            ss = _head_sum_squares(z[:, j:j + 2 * LANES], bd)
            for lo in range(j, min(j + 2 * LANES, width), LANES):
                sl = slice(lo, lo + LANES)
                out_ref[:, sl] = _head_norm_rope(z[:, sl], ss[:, lo - j:lo - j + LANES], gain,
                                                 *tables, scale)

    def values_t(v, n_heads, ch, out_ref):
        vt = v.T.astype(BF16)
        ones = jnp.ones((ONES_ROWS, vt.shape[1]), BF16)
        rows = ch + ONES_ROWS
        for h in range(n_heads):
            out_ref[h * rows:h * rows + ch, :] = vt[h * ch:(h + 1) * ch, :]
            out_ref[h * rows + ch:(h + 1) * rows, :] = ones

    n_q, n_k = GQA_HEADS * HEAD_DIM, GQA_KV * HEAD_DIM
    heads(proj(_OFF_CQ, n_q), n_q, q_o, qn_ref, axial, q_scale)
    kv = proj(_OFF_CK, 2 * n_k)
    heads(kv, n_k, k_o, kn_ref, axial, 1.0)
    values_t(kv[:, n_k:], GQA_KV, HEAD_DIM, vt_o)
    n_d = DIFF_HEADS * 2 * HEAD_DIM
    heads(proj(_OFF_DQ, n_d), n_d, dq_o, dqn_ref, partial, q_scale)
    heads(proj(_OFF_DK, n_d), n_d, dk_o, dkn_ref, partial, 1.0)
    values_t(proj(_OFF_DV, DIFF_HEADS * DIFF_V), DIFF_HEADS, DIFF_V, dvt_o)

    for n in range(N_BRANCH):
        sl = slice(n * D_MODEL, (n + 1) * D_MODEL)
        g = proj(_OFF_GATE + n * D_MODEL, D_MODEL) + bgate_ref[:, sl]
        gate_o[:, sl] = jax.nn.sigmoid(g).astype(BF16)


def _proj(x2, lw, tabs, seq, tm):
    T = x2.shape[0]
    n_s = seq // tm
    tok = lambda w: pl.BlockSpec((tm, w), lambda i: (i, 0))
    tab = pl.BlockSpec((tm, LANES), lambda i: (i % n_s, 0))
    tok_t = lambda w: pl.BlockSpec((w, tm), lambda i: (0, i))
    outs = ((3 * BRANCH_W, False), (BRANCH_W, False), (GQA_HEADS * HEAD_DIM, False),
            (GQA_KV * HEAD_DIM, False), (GQA_KV * (HEAD_DIM + ONES_ROWS), True),
            (DIFF_HEADS * 2 * HEAD_DIM, False), (DIFF_HEADS * 2 * HEAD_DIM, False),
            (DIFF_HEADS * (DIFF_V + ONES_ROWS), True), (N_BRANCH * D_MODEL, False))
    return pl.pallas_call(
        _proj_kernel,
        grid=(T // tm,),
        in_specs=[tok(D_MODEL), _resident((1, D_MODEL)), _resident((D_MODEL, IN_W)),
                  _resident((1, N_BRANCH * D_MODEL)), _resident((2 * LANES, 2 * LANES)),
                  _resident((1, LANES)), _resident((1, LANES)), _resident((1, LANES)),
                  _resident((1, LANES))] + [tab] * 6,
        out_specs=[tok_t(w) if tr else tok(w) for w, tr in outs],
        out_shape=[jax.ShapeDtypeStruct((w, T) if tr else (T, w), BF16) for w, tr in outs],
        compiler_params=_params(1),
        name="proj",
    )(x2, lw["mix_norm"], lw["w_in"], lw["b_gate"], lw["bd"], lw["q_norm"], lw["k_norm"],
      lw["dq_norm"], lw["dk_norm"], *tabs)


def _flash_cols(qst_ref, k_ref, vt_ref, m_ref, acc_ref, s_ref, smax_ref, alpha_ref, p_ref,
                *, tkv, n_kv, pv_groups):
    m_ref[...] = jnp.full(m_ref.shape, NEG_BIG, F32)
    acc_ref[...] = jnp.zeros(acc_ref.shape, F32)

    R = m_ref.shape[1]
    ck = min(MXU_DEPTH, tkv)
    n_chunks = tkv // ck

    def kv_start(u, c):
        base = u * tkv if isinstance(u, int) else pl.multiple_of(u * tkv, tkv)
        return base + c * ck

    def slot(scores_of=None, softmax=False, pv_of=None):
        if pv_of is not None:
            alpha = alpha_ref[...]
        if softmax:
            m_prev = m_ref[...]
            m_new = jnp.maximum(m_prev, smax_ref[...])
            alpha_ref[...] = jnp.exp2(m_prev - m_new)
            m_ref[...] = m_new
        col_max = None
        for c in range(n_chunks):
            rows = slice(c * ck, (c + 1) * ck)
            if pv_of is not None:
                kv = pl.ds(kv_start(pv_of, c), ck)
                for row0, cols in pv_groups:
                    vt = vt_ref[row0:row0 + acc_ref.shape[0], kv]
                    part = jnp.dot(vt, p_ref[rows, cols], preferred_element_type=F32)
                    if c == 0:
                        acc_ref[:, cols] = alpha[:, cols] * acc_ref[:, cols] + part
                    else:
                        acc_ref[:, cols] += part
            if softmax:
                p_ref[rows, :] = jnp.exp2(s_ref[rows, :R] - m_new).astype(BF16)
            if scores_of is not None:
                k = k_ref[0, pl.ds(kv_start(scores_of, c), ck), :]
                s = jnp.dot(k, qst_ref[...], preferred_element_type=F32)
                s_ref[rows, :R] = s
                part = jnp.max(s, axis=0, keepdims=True)
                col_max = part if col_max is None else jnp.maximum(col_max, part)
        if scores_of is not None:
            smax_ref[...] = col_max

    slot(scores_of=0)
    slot(scores_of=1, softmax=True)

    def steady(u, carry):
        slot(scores_of=u, softmax=True, pv_of=u - 2)
        return carry

    lax.fori_loop(2, n_kv, steady, 0, unroll=2 if n_kv - 2 >= 4 else 1)
    slot(softmax=True, pv_of=n_kv - 2)
    slot(pv_of=n_kv - 1)


def _stack_queries(x, qst_ref, lo_cols, hi_cols):
    xt = x.astype(F32).T
    top = lax.broadcasted_iota(jnp.int32, xt.shape, 0) < HEAD_DIM
    zero = jnp.zeros_like(xt)
    qst_ref[:, lo_cols] = jnp.where(top, xt, zero).astype(BF16)
    qst_ref[:, hi_cols] = jnp.where(top, zero, xt).astype(BF16)


def _attn_scratch(R, tkv, acc_rows):
    return [pltpu.VMEM((LANES, R), BF16), pltpu.VMEM((1, R), F32),
            pltpu.VMEM((acc_rows, R + LANES), F32), pltpu.VMEM((tkv, R + LANES), F32),
            pltpu.VMEM((1, R), F32), pltpu.VMEM((1, R), F32),
            pltpu.VMEM((tkv, R), BF16)]


def _gqa_kernel(q_ref, k_ref, vt_ref, o_ref, qst_ref, m_ref, acc_ref, *pipe, tq, tkv, n_kv):
    n_blk = GQA_HEADS // GQA_KV
    col = lambda h: slice(h * tq, (h + 1) * tq)
    for j in range(n_blk):
        _stack_queries(q_ref[0, :, j * LANES:(j + 1) * LANES], qst_ref, col(j), col(n_blk + j))
    half = n_blk * tq
    _flash_cols(qst_ref, k_ref, vt_ref, m_ref, acc_ref, *pipe, tkv=tkv, n_kv=n_kv,
                pv_groups=((0, slice(0, half)), (HEAD_DIM + ONES_ROWS, slice(half, 2 * half))))
    out, den = slice(0, HEAD_DIM), slice(HEAD_DIM, HEAD_DIM + 1)
    for j in range(n_blk):
        c0, c1 = col(j), col(n_blk + j)
        top = acc_ref[out, c0] / acc_ref[den, c0]
        bot = acc_ref[out, c1] / acc_ref[den, c1]
        o_ref[0, :, j * LANES:(j + 1) * LANES] = (
            jnp.concatenate([top, bot], axis=0).T.astype(BF16))


def _gqa(q, k, vt, t):
    B, S, W = q.shape
    tq, tkv = t["gqa_tq"], t["tkv"]
    kern = functools.partial(_gqa_kernel, tq=tq, tkv=tkv, n_kv=S // tkv)
    return pl.pallas_call(
        kern,
        grid=(B, S // tq),
        in_specs=[pl.BlockSpec((1, tq, W), lambda b, i: (b, i, 0)),
                  pl.BlockSpec((1, S, LANES), lambda b, i: (b, 0, 0)),
                  pl.BlockSpec((GQA_KV * (HEAD_DIM + ONES_ROWS), S), lambda b, i: (0, b))],
        out_specs=pl.BlockSpec((1, tq, W), lambda b, i: (b, i, 0)),
        out_shape=jax.ShapeDtypeStruct((B, S, W), BF16),
        scratch_shapes=_attn_scratch(GQA_HEADS * tq, tkv, HEAD_DIM + ONES_ROWS),
        compiler_params=_params(2),
        name="gqa",
    )(q, k, vt)


def _diff_kernel(lam_ref, on_ref, q_ref, k_ref, vt_ref, o_ref, qst_ref, m_ref, acc_ref,
                 *pipe, tq, tkv, n_kv, lambda_init):
    c0, c1 = slice(0, tq), slice(tq, 2 * tq)
    _stack_queries(q_ref[0], qst_ref, c0, c1)
    _flash_cols(qst_ref, k_ref, vt_ref, m_ref, acc_ref, *pipe, tkv=tkv, n_kv=n_kv,
                pv_groups=((0, slice(0, 2 * tq)),))
    lv = lam_ref[...]
    lam = (jnp.exp(jnp.sum(lv[0:1] * lv[1:2], axis=1, keepdims=True))
           - jnp.exp(jnp.sum(lv[2:3] * lv[3:4], axis=1, keepdims=True)) + lambda_init)
    out, den = slice(0, DIFF_V), slice(DIFF_V, DIFF_V + 1)
    ot = (acc_ref[out, c0] / acc_ref[den, c0]
          - lam * (acc_ref[out, c1] / acc_ref[den, c1]))
    o_ref[0] = (_rms(ot.T, on_ref[...]) * (1.0 - lambda_init)).astype(BF16)


def _diff(q, k, vt, lam, out_norm, lambda_init, t):
    B, S, W = q.shape
    tq, tkv = t["diff_tq"], t["tkv"]
    kern = functools.partial(_diff_kernel, tq=tq, tkv=tkv, n_kv=S // tkv,
                             lambda_init=lambda_init)
    return pl.pallas_call(
        kern,
        grid=(B, DIFF_HEADS, S // tq),
        in_specs=[pl.BlockSpec((4, HEAD_DIM), lambda b, h, i: (0, 0)),
                  pl.BlockSpec((1, DIFF_V), lambda b, h, i: (0, 0)),
                  pl.BlockSpec((1, tq, LANES), lambda b, h, i: (b, i, h)),
                  pl.BlockSpec((1, S, LANES), lambda b, h, i: (b, 0, h)),
                  pl.BlockSpec((DIFF_V + ONES_ROWS, S), lambda b, h, i: (h, b))],
        out_specs=pl.BlockSpec((1, tq, LANES), lambda b, h, i: (b, i, h)),
        out_shape=jax.ShapeDtypeStruct((B, S, W), BF16),
        scratch_shapes=_attn_scratch(2 * tq, tkv, DIFF_V + ONES_ROWS),
        compiler_params=_params(3),
        name="diff",
    )(lam, out_norm, q, k, vt)


def _merge_kernel(x_ref, conv_ref, convp_ref, convn_ref, pool_ref, poolp_ref, pooln_ref,
                  og_ref, od_ref, gate_ref, cw_ref, pw_ref, ps_ref, wb_ref, wo_ref, o_ref,
                  cbuf, pbuf, *, tm, n_s, seq):
    si = pl.program_id(0) % n_s
    keep_prev = (si > 0).astype(F32)
    keep_next = (si < n_s - 1).astype(F32)
    W = BRANCH_W

    def gated(ref):
        blk = ref[...].astype(F32)
        return blk[:, 2 * W:3 * W] * blk[:, 0:W]
    cbuf[0:HALO, :] = gated(convp_ref) * keep_prev
    cbuf[HALO:HALO + tm, :] = gated(conv_ref)
    cbuf[HALO + tm:2 * HALO + tm, :] = gated(convn_ref) * keep_next
    cw = cw_ref[...]
    conv = (cw[0:1] * cbuf[HALO - 1:HALO - 1 + tm, :] + cw[1:2] * cbuf[HALO:HALO + tm, :]
            + cw[2:3] * cbuf[HALO + 1:HALO + 1 + tm, :])
    br_a = (conv_ref[:, W:2 * W].astype(F32) * conv).astype(BF16)

    pbuf[0:HALO, :] = poolp_ref[...].astype(F32) * keep_prev
    pbuf[HALO:HALO + tm, :] = pool_ref[...].astype(F32)
    pbuf[HALO + tm:2 * HALO + tm, :] = pooln_ref[...].astype(F32) * keep_next
    pos = si * tm + lax.broadcasted_iota(jnp.int32, (tm, 1), 0)
    pooled = []
    for gi, win in enumerate(POOL_WINDOWS):
        lanes = slice(gi * POOL_GROUP, (gi + 1) * POOL_GROUP)
        half = win // 2
        tot = pbuf[HALO - half:HALO - half + tm, lanes]
        for off in range(-half + 1, half):
            tot = tot + pbuf[HALO + off:HALO + off + tm, lanes]
        cnt = (jnp.clip(pos - half + win, 0, seq) - jnp.clip(pos - half, 0, seq)).astype(F32)
        mean_diff = (tot / cnt - pbuf[HALO:HALO + tm, lanes]).astype(BF16)
        pooled.append(jnp.dot(mean_diff, pw_ref[gi], preferred_element_type=F32))
    br_b = (jnp.concatenate(pooled, axis=1) * ps_ref[...]).astype(BF16)

    merged = None
    for n, br in enumerate((br_a, br_b, og_ref[...], od_ref[...])):
        y = jnp.dot(br, wb_ref[n], preferred_element_type=F32)
        y = gate_ref[:, n * D_MODEL:(n + 1) * D_MODEL].astype(F32) * y
        merged = y if merged is None else merged + y
    o_ref[...] = x_ref[...] + jnp.dot(merged.astype(BF16), wo_ref[...],
                                      preferred_element_type=F32)


def _merge(x2, conv, pool, og, od, gates, lw, seq, tm):
    T = x2.shape[0]
    n_s = seq // tm
    per = tm // HALO
    last = T // HALO - 1
    tok = lambda w: pl.BlockSpec((tm, w), lambda i: (i, 0))
    prev = lambda w: pl.BlockSpec((HALO, w), lambda i: (jnp.maximum(i * per - 1, 0), 0))
    nxt = lambda w: pl.BlockSpec((HALO, w), lambda i: (jnp.minimum((i + 1) * per, last), 0))
    kern = functools.partial(_merge_kernel, tm=tm, n_s=n_s, seq=seq)
    return pl.pallas_call(
        kern,
        grid=(T // tm,),
        in_specs=[tok(D_MODEL),
                  tok(3 * BRANCH_W), prev(3 * BRANCH_W), nxt(3 * BRANCH_W),
                  tok(BRANCH_W), prev(BRANCH_W), nxt(BRANCH_W),
                  tok(BRANCH_W), tok(BRANCH_W), tok(N_BRANCH * D_MODEL),
                  _resident((3, BRANCH_W)), _resident((4, POOL_GROUP, POOL_GROUP)),
                  _resident((1, BRANCH_W)), _resident((N_BRANCH, BRANCH_W, D_MODEL)),
                  _resident((D_MODEL, D_MODEL))],
        out_specs=tok(D_MODEL),
        out_shape=jax.ShapeDtypeStruct((T, D_MODEL), F32),
        scratch_shapes=[pltpu.VMEM((tm + 2 * HALO, BRANCH_W), F32),
                        pltpu.VMEM((tm + 2 * HALO, BRANCH_W), F32)],
        compiler_params=_params(1),
        name="merge",
    )(x2, conv, conv, conv, pool, pool, pool, og, od, gates,
      lw["conv_w"], lw["pool_w"], lw["pool_scale"], lw["w_branch"], lw["w_out"])


def _gqa_lane_perm():
    n_blk = GQA_HEADS // GQA_KV
    perm = np.empty(GQA_HEADS * HEAD_DIM, np.int32)
    for j in range(n_blk):
        for g in range(GQA_KV):
            dst = j * LANES + g * HEAD_DIM
            src = (g * n_blk + j) * HEAD_DIM
            perm[dst:dst + HEAD_DIM] = np.arange(src, src + HEAD_DIM)
    return perm


def _rope_tables(seq):
    def freqs(n, theta):
        half = n // 2
        return jnp.exp(-math.log(theta) * jnp.arange(half, dtype=F32) * (2.0 / n))

    def angles(pos, n, theta):
        return pos.astype(F32)[:, None] * freqs(n, theta)[None, :]

    def rotary(ang):
        c, s, z = jnp.cos(ang), jnp.sin(ang), jnp.zeros_like(ang)
        return (jnp.concatenate([c, c], 1), jnp.concatenate([-s, z], 1),
                jnp.concatenate([z, s], 1))

    rows = seq // GRID_W
    row = jnp.repeat(jnp.arange(rows), GRID_W)
    col = jnp.tile(jnp.arange(GRID_W), rows)
    half_dim = HEAD_DIM // 2
    ax = [jnp.concatenate([a, b], 1) for a, b in zip(rotary(angles(row, half_dim, AXIAL_THETA)),
                                                     rotary(angles(col, half_dim, AXIAL_THETA)))]
    rest = HEAD_DIM - ROPE_DIM
    pr = rotary(angles(jnp.arange(seq), ROPE_DIM, ROPE_THETA))
    pt = [jnp.concatenate([pr[0], jnp.ones((seq, rest), F32)], 1),
          jnp.concatenate([pr[1], jnp.zeros((seq, rest), F32)], 1),
          jnp.concatenate([pr[2], jnp.zeros((seq, rest), F32)], 1)]
    two = lambda a: jnp.concatenate([a, a], 1)
    return tuple(two(a) for a in ax) + tuple(two(a) for a in pt)


def _layer_weights(i, ffn1_norm, ffn1_w_in, ffn1_w_out, mix_norm, w_in, b_gate, conv_w, pool_w,
                   pool_scale, attn_q_norm, attn_k_norm, diff_q_norm, diff_k_norm, diff_lambda,
                   diff_out_norm, w_branch, w_out, ffn2_norm, ffn2_w_in, ffn2_w_out):
    perm = _gqa_lane_perm()
    wi = w_in[i]
    wi = wi.at[:, _OFF_CQ:_OFF_CK].set(wi[:, _OFF_CQ:_OFF_CK][:, perm])
    wb = w_branch[i]
    wb = wb.at[2].set(wb[2][perm, :])
    two = lambda g: jnp.concatenate([g, g]).reshape(1, LANES).astype(F32)
    blk = np.arange(2 * LANES) // HEAD_DIM
    return dict(
        ffn1=(ffn1_norm[i].reshape(1, D_MODEL), ffn1_w_in[i].astype(BF16),
              ffn1_w_out[i].astype(BF16)),
        ffn2=(ffn2_norm[i].reshape(1, D_MODEL), ffn2_w_in[i].astype(BF16),
              ffn2_w_out[i].astype(BF16)),
        mix_norm=mix_norm[i].reshape(1, D_MODEL),
        w_in=wi.astype(BF16),
        b_gate=b_gate[i].reshape(1, N_BRANCH * D_MODEL),
        bd=jnp.asarray(blk[:, None] == blk[None, :], BF16),
        q_norm=two(attn_q_norm[i]), k_norm=two(attn_k_norm[i]),
        dq_norm=two(diff_q_norm[i]), dk_norm=two(diff_k_norm[i]),
        diff_lambda=diff_lambda[i].astype(F32),
        diff_out_norm=diff_out_norm[i].reshape(1, DIFF_V),
        lambda_init=0.8 - 0.6 * math.exp(-0.3 * i),
        conv_w=conv_w[i], pool_w=pool_w[i].astype(BF16),
        pool_scale=pool_scale[i].reshape(1, BRANCH_W),
        w_branch=wb.astype(BF16), w_out=w_out[i].astype(BF16),
    )


def _trunk(x, layers):
    B, S, _ = x.shape
    T = B * S
    t = _tiles(S)
    tabs = _rope_tables(S)
    x2 = x.reshape(T, D_MODEL)
    for lw in layers:
        x2 = _ffn(x2, *lw["ffn1"], t["ffn_tm"])
        conv, pool, q, k, vt, dq, dk, dvt, gates = _proj(x2, lw, tabs, S, t["proj_tm"])
        seq3 = lambda a: a.reshape(B, S, a.shape[-1])
        og = _gqa(seq3(q), seq3(k), vt, t)
        od = _diff(seq3(dq), seq3(dk), dvt, lw["diff_lambda"], lw["diff_out_norm"],
                   lw["lambda_init"], t)
        x2 = _merge(x2, conv, pool, og.reshape(T, -1), od.reshape(T, -1), gates, lw, S,
                    t["merge_tm"])
        x2 = _ffn(x2, *lw["ffn2"], t["ffn_tm"])
    return x2.reshape(B, S, D_MODEL)


def kernel(x_prompt, x_sample, ffn1_norm, ffn1_w_in, ffn1_w_out, mix_norm, w_in, b_gate, conv_w, pool_w, pool_scale, attn_q_norm, attn_k_norm, diff_q_norm, diff_k_norm, diff_lambda, diff_out_norm, w_branch, w_out, ffn2_norm, ffn2_w_in, ffn2_w_out):
    weights = (ffn1_norm, ffn1_w_in, ffn1_w_out, mix_norm, w_in, b_gate, conv_w, pool_w,
               pool_scale, attn_q_norm, attn_k_norm, diff_q_norm, diff_k_norm, diff_lambda,
               diff_out_norm, w_branch, w_out, ffn2_norm, ffn2_w_in, ffn2_w_out)
    layers = [_layer_weights(i, *weights) for i in range(DEPTH)]
    return (_trunk(x_prompt, layers), _trunk(x_sample, layers))
```

```python
import functools
import math

import jax
import jax.numpy as jnp
import numpy as np
from jax import lax
from jax.experimental import pallas as pl
from jax.experimental.pallas import tpu as pltpu

F32 = jnp.float32
BF16 = jnp.bfloat16

D_MODEL = 1024
DEPTH = 2
GRID_W = 64
HEAD_DIM = 64
D_FF = 2816
N_BRANCH = 4
BRANCH_W = 512
POOL_WINDOWS = (2, 4, 8, 16)
POOL_GROUP = BRANCH_W // 4
GQA_HEADS = 8
GQA_KV = 2
DIFF_HEADS = 4
DIFF_V = 2 * HEAD_DIM
AXIAL_THETA = 10000.0
ROPE_THETA = 500000.0
ROPE_DIM = HEAD_DIM // 4
EPS = 1e-6

_OFF_CONV = 0
_OFF_POOL = 3 * BRANCH_W
_OFF_CQ = _OFF_POOL + BRANCH_W
_OFF_CK = _OFF_CQ + GQA_HEADS * HEAD_DIM
_OFF_CV = _OFF_CK + GQA_KV * HEAD_DIM
_OFF_DQ = _OFF_CV + GQA_KV * HEAD_DIM
_OFF_DK = _OFF_DQ + DIFF_HEADS * 2 * HEAD_DIM
_OFF_DV = _OFF_DK + DIFF_HEADS * 2 * HEAD_DIM
_OFF_GATE = _OFF_DV + DIFF_HEADS * DIFF_V
IN_W = _OFF_GATE + N_BRANCH * D_MODEL

LANES = 128
BF16_SUBLANES = 16
VMEM_LIMIT_BYTES = 56 * 1024 * 1024

HALO = BF16_SUBLANES
ONES_ROWS = BF16_SUBLANES
NEG_BIG = -1e30
MXU_DEPTH = 256
SCORE_TILE_ELEMS = 1024 * 2048


def _tiles(seq):
    tkv = min(1024, seq // 4)
    cols = SCORE_TILE_ELEMS // tkv
    return dict(
        ffn_tm=512,
        proj_tm=256,
        merge_tm=512,
        gqa_tq=min(cols // GQA_HEADS, seq),
        diff_tq=min(cols // 2, seq),
        tkv=tkv,
    )


def _resident(shape):
    zeros = (0,) * len(shape)
    return pl.BlockSpec(shape, lambda *_: zeros, pipeline_mode=pl.Buffered(1))


def _params(n_axes):
    return pltpu.CompilerParams(dimension_semantics=("arbitrary",) * n_axes,
                                vmem_limit_bytes=VMEM_LIMIT_BYTES)


def _rms(xf, gain):
    return xf * lax.rsqrt(jnp.mean(xf * xf, axis=-1, keepdims=True) + EPS) * gain


def _ffn_kernel(x_ref, g_ref, win_ref, wout_ref, o_ref):
    x = x_ref[...]
    u = _rms(x, g_ref[...]).astype(BF16)
    h = jnp.dot(u, win_ref[...], preferred_element_type=F32)
    a = h[:, :D_FF]
    b = h[:, D_FF:]
    act = (a * jax.nn.sigmoid(a) * b).astype(BF16)
    y = jnp.dot(act, wout_ref[...], preferred_element_type=F32)
    o_ref[...] = x + 0.5 * y


def _ffn(x2, gain, w_in, w_out, tm):
    T = x2.shape[0]
    return pl.pallas_call(
        _ffn_kernel,
        grid=(T // tm,),
        in_specs=[pl.BlockSpec((tm, D_MODEL), lambda i: (i, 0)),
                  _resident((1, D_MODEL)),
                  _resident((D_MODEL, 2 * D_FF)),
                  _resident((D_FF, D_MODEL))],
        out_specs=pl.BlockSpec((tm, D_MODEL), lambda i: (i, 0)),
        out_shape=jax.ShapeDtypeStruct((T, D_MODEL), F32),
        compiler_params=_params(1),
        name="ffn",
    )(x2, gain, w_in, w_out)


def _head_sum_squares(z, bd):
    return jnp.dot((z * z).astype(BF16), bd, preferred_element_type=F32)


def _head_norm_rope(z, ss, gain, cos, sin_a, sin_b, shift, scale):
    zn = z * lax.rsqrt(ss * (1.0 / HEAD_DIM) + EPS) * gain
    out = zn * cos + pltpu.roll(zn, LANES - shift, 1) * sin_a + pltpu.roll(zn, shift, 1) * sin_b
    if scale != 1.0:
        out = out * scale
    return out.astype(BF16)


def _proj_kernel(x_ref, g_ref, w_ref, bgate_ref, bd_ref, qn_ref, kn_ref, dqn_ref, dkn_ref,
                 acos_ref, asa_ref, asb_ref, pcos_ref, psa_ref, psb_ref,
                 conv_o, pool_o, q_o, k_o, vt_o, dq_o, dk_o, dvt_o, gate_o):
    u = _rms(x_ref[...], g_ref[...]).astype(BF16)

    def proj(lo, width):
        return jnp.dot(u, w_ref[:, lo:lo + width], preferred_element_type=F32)

    conv_o[...] = proj(_OFF_CONV, 3 * BRANCH_W).astype(BF16)
    pool_o[...] = proj(_OFF_POOL, BRANCH_W).astype(BF16)

    bd = bd_ref[...]
    axial = (acos_ref[...], asa_ref[...], asb_ref[...], HEAD_DIM // 4)
    partial = (pcos_ref[...], psa_ref[...], psb_ref[...], ROPE_DIM // 2)
    q_scale = HEAD_DIM ** -0.5 * math.log2(math.e)

    def heads(z, width, out_ref, gain_ref, tables, scale):
        gain = gain_ref[...]
        for j in range(0, width, 2 * LANES):
            ss = _head_sum_squares(z[:, j:j + 2 * LANES], bd)
            for lo in range(j, min(j + 2 * LANES, width), LANES):
                sl = slice(lo, lo + LANES)
                out_ref[:, sl] = _head_norm_rope(z[:, sl], ss[:, lo - j:lo - j + LANES], gain,
                                                 *tables, scale)

    def values_t(v, n_heads, ch, out_ref):
        vt = v.T.astype(BF16)
        ones = jnp.ones((ONES_ROWS, vt.shape[1]), BF16)
        rows = ch + ONES_ROWS
        for h in range(n_heads):
            out_ref[h * rows:h * rows + ch, :] = vt[h * ch:(h + 1) * ch, :]
            out_ref[h * rows + ch:(h + 1) * rows, :] = ones

    n_q, n_k = GQA_HEADS * HEAD_DIM, GQA_KV * HEAD_DIM
    heads(proj(_OFF_CQ, n_q), n_q, q_o, qn_ref, axial, q_scale)
    kv = proj(_OFF_CK, 2 * n_k)
    heads(kv, n_k, k_o, kn_ref, axial, 1.0)
    values_t(kv[:, n_k:], GQA_KV, HEAD_DIM, vt_o)
    n_d = DIFF_HEADS * 2 * HEAD_DIM
    heads(proj(_OFF_DQ, n_d), n_d, dq_o, dqn_ref, partial, q_scale)
    heads(proj(_OFF_DK, n_d), n_d, dk_o, dkn_ref, partial, 1.0)
    values_t(proj(_OFF_DV, DIFF_HEADS * DIFF_V), DIFF_HEADS, DIFF_V, dvt_o)

    for n in range(N_BRANCH):
        sl = slice(n * D_MODEL, (n + 1) * D_MODEL)
        g = proj(_OFF_GATE + n * D_MODEL, D_MODEL) + bgate_ref[:, sl]
        gate_o[:, sl] = jax.nn.sigmoid(g).astype(BF16)


def _proj(x2, lw, tabs, seq, tm):
    T = x2.shape[0]
    n_s = seq // tm
    tok = lambda w: pl.BlockSpec((tm, w), lambda i: (i, 0))
    tab = pl.BlockSpec((tm, LANES), lambda i: (i % n_s, 0))
    tok_t = lambda w: pl.BlockSpec((w, tm), lambda i: (0, i))
    outs = ((3 * BRANCH_W, False), (BRANCH_W, False), (GQA_HEADS * HEAD_DIM, False),
            (GQA_KV * HEAD_DIM, False), (GQA_KV * (HEAD_DIM + ONES_ROWS), True),
            (DIFF_HEADS * 2 * HEAD_DIM, False), (DIFF_HEADS * 2 * HEAD_DIM, False),
            (DIFF_HEADS * (DIFF_V + ONES_ROWS), True), (N_BRANCH * D_MODEL, False))
    return pl.pallas_call(
        _proj_kernel,
        grid=(T // tm,),
        in_specs=[tok(D_MODEL), _resident((1, D_MODEL)), _resident((D_MODEL, IN_W)),
                  _resident((1, N_BRANCH * D_MODEL)), _resident((2 * LANES, 2 * LANES)),
                  _resident((1, LANES)), _resident((1, LANES)), _resident((1, LANES)),
                  _resident((1, LANES))] + [tab] * 6,
        out_specs=[tok_t(w) if tr else tok(w) for w, tr in outs],
        out_shape=[jax.ShapeDtypeStruct((w, T) if tr else (T, w), BF16) for w, tr in outs],
        compiler_params=_params(1),
        name="proj",
    )(x2, lw["mix_norm"], lw["w_in"], lw["b_gate"], lw["bd"], lw["q_norm"], lw["k_norm"],
      lw["dq_norm"], lw["dk_norm"], *tabs)


def _flash_cols(qst_ref, k_ref, vt_ref, m_ref, acc_ref, s_ref, smax_ref, alpha_ref, p_ref,
                *, tkv, n_kv, pv_groups):
    m_ref[...] = jnp.full(m_ref.shape, NEG_BIG, F32)
    acc_ref[...] = jnp.zeros(acc_ref.shape, F32)

    R = m_ref.shape[1]
    ck = min(MXU_DEPTH, tkv)
    n_chunks = tkv // ck

    def kv_start(u, c):
        base = u * tkv if isinstance(u, int) else pl.multiple_of(u * tkv, tkv)
        return base + c * ck

    def slot(scores_of=None, softmax=False, pv_of=None):
        if pv_of is not None:
            alpha = alpha_ref[...]
        if softmax:
            m_prev = m_ref[...]
            m_new = jnp.maximum(m_prev, smax_ref[...])
            alpha_ref[...] = jnp.exp2(m_prev - m_new)
            m_ref[...] = m_new
        col_max = None
        for c in range(n_chunks):
            rows = slice(c * ck, (c + 1) * ck)
            if pv_of is not None:
                kv = pl.ds(kv_start(pv_of, c), ck)
                for row0, cols in pv_groups:
                    vt = vt_ref[row0:row0 + acc_ref.shape[0], kv]
                    p = jnp.concatenate([p_ref[j, rows, :] for j in
                                         range(cols.start // LANES, cols.stop // LANES)], axis=1)
                    part = jnp.dot(vt, p, preferred_element_type=F32)
                    if c == 0:
                        acc_ref[:, cols] = alpha[:, cols] * acc_ref[:, cols] + part
                    else:
                        acc_ref[:, cols] += part
            if softmax:
                p = jnp.exp2(s_ref[rows, :R] - m_new).astype(BF16)
                for j in range(R // LANES):
                    p_ref[j, rows, :] = p[:, j * LANES:(j + 1) * LANES]
            if scores_of is not None:
                k = k_ref[0, pl.ds(kv_start(scores_of, c), ck), :]
                s = jnp.dot(k, qst_ref[...], preferred_element_type=F32)
                s_ref[rows, :R] = s
                part = jnp.max(s, axis=0, keepdims=True)
                col_max = part if col_max is None else jnp.maximum(col_max, part)
        if scores_of is not None:
            smax_ref[...] = col_max

    slot(scores_of=0)
    slot(scores_of=1, softmax=True)

    def steady(u, carry):
        slot(scores_of=u, softmax=True, pv_of=u - 2)
        return carry

    lax.fori_loop(2, n_kv, steady, 0, unroll=2 if n_kv - 2 >= 4 else 1)
    slot(softmax=True, pv_of=n_kv - 2)
    slot(pv_of=n_kv - 1)


def _stack_queries(x, qst_ref, lo_cols, hi_cols):
    xt = x.astype(F32).T
    top = lax.broadcasted_iota(jnp.int32, xt.shape, 0) < HEAD_DIM
    zero = jnp.zeros_like(xt)
    qst_ref[:, lo_cols] = jnp.where(top, xt, zero).astype(BF16)
    qst_ref[:, hi_cols] = jnp.where(top, zero, xt).astype(BF16)


def _attn_scratch(R, tkv, acc_rows):
    return [pltpu.VMEM((LANES, R), BF16), pltpu.VMEM((1, R), F32),
            pltpu.VMEM((acc_rows, R + LANES), F32), pltpu.VMEM((tkv, R + LANES), F32),
            pltpu.VMEM((1, R), F32), pltpu.VMEM((1, R), F32),
            pltpu.VMEM((R // LANES, tkv, LANES), BF16)]


def _gqa_kernel(q_ref, k_ref, vt_ref, o_ref, qst_ref, m_ref, acc_ref, *pipe, tq, tkv, n_kv):
    n_blk = GQA_HEADS // GQA_KV
    col = lambda h: slice(h * tq, (h + 1) * tq)
    for j in range(n_blk):
        _stack_queries(q_ref[0, :, j * LANES:(j + 1) * LANES], qst_ref, col(j), col(n_blk + j))
    half = n_blk * tq
    _flash_cols(qst_ref, k_ref, vt_ref, m_ref, acc_ref, *pipe, tkv=tkv, n_kv=n_kv,
                pv_groups=((0, slice(0, half)), (HEAD_DIM + ONES_ROWS, slice(half, 2 * half))))
    out, den = slice(0, HEAD_DIM), slice(HEAD_DIM, HEAD_DIM + 1)
    for j in range(n_blk):
        c0, c1 = col(j), col(n_blk + j)
        top = acc_ref[out, c0] / acc_ref[den, c0]
        bot = acc_ref[out, c1] / acc_ref[den, c1]
        o_ref[0, :, j * LANES:(j + 1) * LANES] = (
            jnp.concatenate([top, bot], axis=0).T.astype(BF16))


def _gqa(q, k, vt, t):
    B, S, W = q.shape
    tq, tkv = t["gqa_tq"], t["tkv"]
    kern = functools.partial(_gqa_kernel, tq=tq, tkv=tkv, n_kv=S // tkv)
    return pl.pallas_call(
        kern,
        grid=(B, S // tq),
        in_specs=[pl.BlockSpec((1, tq, W), lambda b, i: (b, i, 0)),
                  pl.BlockSpec((1, S, LANES), lambda b, i: (b, 0, 0)),
                  pl.BlockSpec((GQA_KV * (HEAD_DIM + ONES_ROWS), S), lambda b, i: (0, b))],
        out_specs=pl.BlockSpec((1, tq, W), lambda b, i: (b, i, 0)),
        out_shape=jax.ShapeDtypeStruct((B, S, W), BF16),
        scratch_shapes=_attn_scratch(GQA_HEADS * tq, tkv, HEAD_DIM + ONES_ROWS),
        compiler_params=_params(2),
        name="gqa",
    )(q, k, vt)


def _diff_kernel(lam_ref, on_ref, q_ref, k_ref, vt_ref, o_ref, qst_ref, m_ref, acc_ref,
                 *pipe, tq, tkv, n_kv, lambda_init):
    c0, c1 = slice(0, tq), slice(tq, 2 * tq)
    _stack_queries(q_ref[0], qst_ref, c0, c1)
    _flash_cols(qst_ref, k_ref, vt_ref, m_ref, acc_ref, *pipe, tkv=tkv, n_kv=n_kv,
                pv_groups=((0, slice(0, 2 * tq)),))
    lv = lam_ref[...]
    lam = (jnp.exp(jnp.sum(lv[0:1] * lv[1:2], axis=1, keepdims=True))
           - jnp.exp(jnp.sum(lv[2:3] * lv[3:4], axis=1, keepdims=True)) + lambda_init)
    out, den = slice(0, DIFF_V), slice(DIFF_V, DIFF_V + 1)
    ot = (acc_ref[out, c0] / acc_ref[den, c0]
          - lam * (acc_ref[out, c1] / acc_ref[den, c1]))
    o_ref[0] = (_rms(ot.T, on_ref[...]) * (1.0 - lambda_init)).astype(BF16)


def _diff(q, k, vt, lam, out_norm, lambda_init, t):
    B, S, W = q.shape
    tq, tkv = t["diff_tq"], t["tkv"]
    kern = functools.partial(_diff_kernel, tq=tq, tkv=tkv, n_kv=S // tkv,
                             lambda_init=lambda_init)
    return pl.pallas_call(
        kern,
        grid=(B, DIFF_HEADS, S // tq),
        in_specs=[pl.BlockSpec((4, HEAD_DIM), lambda b, h, i: (0, 0)),
                  pl.BlockSpec((1, DIFF_V), lambda b, h, i: (0, 0)),
                  pl.BlockSpec((1, tq, LANES), lambda b, h, i: (b, i, h)),
                  pl.BlockSpec((1, S, LANES), lambda b, h, i: (b, 0, h)),
                  pl.BlockSpec((DIFF_V + ONES_ROWS, S), lambda b, h, i: (h, b))],
        out_specs=pl.BlockSpec((1, tq, LANES), lambda b, h, i: (b, i, h)),
        out_shape=jax.ShapeDtypeStruct((B, S, W), BF16),
        scratch_shapes=_attn_scratch(2 * tq, tkv, DIFF_V + ONES_ROWS),
        compiler_params=_params(3),
        name="diff",
    )(lam, out_norm, q, k, vt)


def _merge_kernel(x_ref, conv_ref, convp_ref, convn_ref, pool_ref, poolp_ref, pooln_ref,
                  og_ref, od_ref, gate_ref, cw_ref, pw_ref, ps_ref, wb_ref, wo_ref, o_ref,
                  cbuf, pbuf, *, tm, n_s, seq):
    si = pl.program_id(0) % n_s
    keep_prev = (si > 0).astype(F32)
    keep_next = (si < n_s - 1).astype(F32)
    W = BRANCH_W

    def gated(ref):
        blk = ref[...].astype(F32)
        return blk[:, 2 * W:3 * W] * blk[:, 0:W]
    cbuf[0:HALO, :] = gated(convp_ref) * keep_prev
    cbuf[HALO:HALO + tm, :] = gated(conv_ref)
    cbuf[HALO + tm:2 * HALO + tm, :] = gated(convn_ref) * keep_next
    cw = cw_ref[...]
    conv = (cw[0:1] * cbuf[HALO - 1:HALO - 1 + tm, :] + cw[1:2] * cbuf[HALO:HALO + tm, :]
            + cw[2:3] * cbuf[HALO + 1:HALO + 1 + tm, :])
    br_a = (conv_ref[:, W:2 * W].astype(F32) * conv).astype(BF16)

    pbuf[0:HALO, :] = poolp_ref[...].astype(F32) * keep_prev
    pbuf[HALO:HALO + tm, :] = pool_ref[...].astype(F32)
    pbuf[HALO + tm:2 * HALO + tm, :] = pooln_ref[...].astype(F32) * keep_next
    pos = si * tm + lax.broadcasted_iota(jnp.int32, (tm, 1), 0)
    pooled = []
    for gi, win in enumerate(POOL_WINDOWS):
        lanes = slice(gi * POOL_GROUP, (gi + 1) * POOL_GROUP)
        half = win // 2
        tot = pbuf[HALO - half:HALO - half + tm, lanes]
        for off in range(-half + 1, half):
            tot = tot + pbuf[HALO + off:HALO + off + tm, lanes]
        cnt = (jnp.clip(pos - half + win, 0, seq) - jnp.clip(pos - half, 0, seq)).astype(F32)
        mean_diff = (tot / cnt - pbuf[HALO:HALO + tm, lanes]).astype(BF16)
        pooled.append(jnp.dot(mean_diff, pw_ref[gi], preferred_element_type=F32))
    br_b = (jnp.concatenate(pooled, axis=1) * ps_ref[...]).astype(BF16)

    merged = None
    for n, br in enumerate((br_a, br_b, og_ref[...], od_ref[...])):
        y = jnp.dot(br, wb_ref[n], preferred_element_type=F32)
        y = gate_ref[:, n * D_MODEL:(n + 1) * D_MODEL].astype(F32) * y
        merged = y if merged is None else merged + y
    o_ref[...] = x_ref[...] + jnp.dot(merged.astype(BF16), wo_ref[...],
                                      preferred_element_type=F32)


def _merge(x2, conv, pool, og, od, gates, lw, seq, tm):
    T = x2.shape[0]
    n_s = seq // tm
    per = tm // HALO
    last = T // HALO - 1
    tok = lambda w: pl.BlockSpec((tm, w), lambda i: (i, 0))
    prev = lambda w: pl.BlockSpec((HALO, w), lambda i: (jnp.maximum(i * per - 1, 0), 0))
    nxt = lambda w: pl.BlockSpec((HALO, w), lambda i: (jnp.minimum((i + 1) * per, last), 0))
    kern = functools.partial(_merge_kernel, tm=tm, n_s=n_s, seq=seq)
    return pl.pallas_call(
        kern,
        grid=(T // tm,),
        in_specs=[tok(D_MODEL),
                  tok(3 * BRANCH_W), prev(3 * BRANCH_W), nxt(3 * BRANCH_W),
                  tok(BRANCH_W), prev(BRANCH_W), nxt(BRANCH_W),
                  tok(BRANCH_W), tok(BRANCH_W), tok(N_BRANCH * D_MODEL),
                  _resident((3, BRANCH_W)), _resident((4, POOL_GROUP, POOL_GROUP)),
                  _resident((1, BRANCH_W)), _resident((N_BRANCH, BRANCH_W, D_MODEL)),
                  _resident((D_MODEL, D_MODEL))],
        out_specs=tok(D_MODEL),
        out_shape=jax.ShapeDtypeStruct((T, D_MODEL), F32),
        scratch_shapes=[pltpu.VMEM((tm + 2 * HALO, BRANCH_W), F32),
                        pltpu.VMEM((tm + 2 * HALO, BRANCH_W), F32)],
        compiler_params=_params(1),
        name="merge",
    )(x2, conv, conv, conv, pool, pool, pool, og, od, gates,
      lw["conv_w"], lw["pool_w"], lw["pool_scale"], lw["w_branch"], lw["w_out"])


def _gqa_lane_perm():
    n_blk = GQA_HEADS // GQA_KV
    perm = np.empty(GQA_HEADS * HEAD_DIM, np.int32)
    for j in range(n_blk):
        for g in range(GQA_KV):
            dst = j * LANES + g * HEAD_DIM
            src = (g * n_blk + j) * HEAD_DIM
            perm[dst:dst + HEAD_DIM] = np.arange(src, src + HEAD_DIM)
    return perm


def _rope_tables(seq):
    def freqs(n, theta):
        half = n // 2
        return jnp.exp(-math.log(theta) * jnp.arange(half, dtype=F32) * (2.0 / n))

    def angles(pos, n, theta):
        return pos.astype(F32)[:, None] * freqs(n, theta)[None, :]

    def rotary(ang):
        c, s, z = jnp.cos(ang), jnp.sin(ang), jnp.zeros_like(ang)
        return (jnp.concatenate([c, c], 1), jnp.concatenate([-s, z], 1),
                jnp.concatenate([z, s], 1))

    rows = seq // GRID_W
    row = jnp.repeat(jnp.arange(rows), GRID_W)
    col = jnp.tile(jnp.arange(GRID_W), rows)
    half_dim = HEAD_DIM // 2
    ax = [jnp.concatenate([a, b], 1) for a, b in zip(rotary(angles(row, half_dim, AXIAL_THETA)),
                                                     rotary(angles(col, half_dim, AXIAL_THETA)))]
    rest = HEAD_DIM - ROPE_DIM
    pr = rotary(angles(jnp.arange(seq), ROPE_DIM, ROPE_THETA))
    pt = [jnp.concatenate([pr[0], jnp.ones((seq, rest), F32)], 1),
          jnp.concatenate([pr[1], jnp.zeros((seq, rest), F32)], 1),
          jnp.concatenate([pr[2], jnp.zeros((seq, rest), F32)], 1)]
    two = lambda a: jnp.concatenate([a, a], 1)
    return tuple(two(a) for a in ax) + tuple(two(a) for a in pt)


def _layer_weights(i, ffn1_norm, ffn1_w_in, ffn1_w_out, mix_norm, w_in, b_gate, conv_w, pool_w,
                   pool_scale, attn_q_norm, attn_k_norm, diff_q_norm, diff_k_norm, diff_lambda,
                   diff_out_norm, w_branch, w_out, ffn2_norm, ffn2_w_in, ffn2_w_out):
    perm = _gqa_lane_perm()
    wi = w_in[i]
    wi = wi.at[:, _OFF_CQ:_OFF_CK].set(wi[:, _OFF_CQ:_OFF_CK][:, perm])
    wb = w_branch[i]
    wb = wb.at[2].set(wb[2][perm, :])
    two = lambda g: jnp.concatenate([g, g]).reshape(1, LANES).astype(F32)
    blk = np.arange(2 * LANES) // HEAD_DIM
    return dict(
        ffn1=(ffn1_norm[i].reshape(1, D_MODEL), ffn1_w_in[i].astype(BF16),
              ffn1_w_out[i].astype(BF16)),
        ffn2=(ffn2_norm[i].reshape(1, D_MODEL), ffn2_w_in[i].astype(BF16),
              ffn2_w_out[i].astype(BF16)),
        mix_norm=mix_norm[i].reshape(1, D_MODEL),
        w_in=wi.astype(BF16),
        b_gate=b_gate[i].reshape(1, N_BRANCH * D_MODEL),
        bd=jnp.asarray(blk[:, None] == blk[None, :], BF16),
        q_norm=two(attn_q_norm[i]), k_norm=two(attn_k_norm[i]),
        dq_norm=two(diff_q_norm[i]), dk_norm=two(diff_k_norm[i]),
        diff_lambda=diff_lambda[i].astype(F32),
        diff_out_norm=diff_out_norm[i].reshape(1, DIFF_V),
        lambda_init=0.8 - 0.6 * math.exp(-0.3 * i),
        conv_w=conv_w[i], pool_w=pool_w[i].astype(BF16),
        pool_scale=pool_scale[i].reshape(1, BRANCH_W),
        w_branch=wb.astype(BF16), w_out=w_out[i].astype(BF16),
    )


def _trunk(x, layers):
    B, S, _ = x.shape
    T = B * S
    t = _tiles(S)
    tabs = _rope_tables(S)
    x2 = x.reshape(T, D_MODEL)
    for lw in layers:
        x2 = _ffn(x2, *lw["ffn1"], t["ffn_tm"])
        conv, pool, q, k, vt, dq, dk, dvt, gates = _proj(x2, lw, tabs, S, t["proj_tm"])
        seq3 = lambda a: a.reshape(B, S, a.shape[-1])
        og = _gqa(seq3(q), seq3(k), vt, t)
        od = _diff(seq3(dq), seq3(dk), dvt, lw["diff_lambda"], lw["diff_out_norm"],
                   lw["lambda_init"], t)
        x2 = _merge(x2, conv, pool, og.reshape(T, -1), od.reshape(T, -1), gates, lw, S,
                    t["merge_tm"])
        x2 = _ffn(x2, *lw["ffn2"], t["ffn_tm"])
    return x2.reshape(B, S, D_MODEL)


def kernel(x_prompt, x_sample, ffn1_norm, ffn1_w_in, ffn1_w_out, mix_norm, w_in, b_gate, conv_w, pool_w, pool_scale, attn_q_norm, attn_k_norm, diff_q_norm, diff_k_norm, diff_lambda, diff_out_norm, w_branch, w_out, ffn2_norm, ffn2_w_in, ffn2_w_out):
    weights = (ffn1_norm, ffn1_w_in, ffn1_w_out, mix_norm, w_in, b_gate, conv_w, pool_w,
               pool_scale, attn_q_norm, attn_k_norm, diff_q_norm, diff_k_norm, diff_lambda,
               diff_out_norm, w_branch, w_out, ffn2_norm, ffn2_w_in, ffn2_w_out)
    layers = [_layer_weights(i, *weights) for i in range(DEPTH)]
    return (_trunk(x_prompt, layers), _trunk(x_sample, layers))
```

```python
import functools
import math

import jax
import jax.numpy as jnp
import numpy as np
from jax import lax
from jax.experimental import pallas as pl
from jax.experimental.pallas import tpu as pltpu

F32 = jnp.float32
BF16 = jnp.bfloat16

D_MODEL = 1024
DEPTH = 2
GRID_W = 64
HEAD_DIM = 64
D_FF = 2816
N_BRANCH = 4
BRANCH_W = 512
POOL_WINDOWS = (2, 4, 8, 16)
POOL_GROUP = BRANCH_W // 4
GQA_HEADS = 8
GQA_KV = 2
DIFF_HEADS = 4
DIFF_V = 2 * HEAD_DIM
AXIAL_THETA = 10000.0
ROPE_THETA = 500000.0
ROPE_DIM = HEAD_DIM // 4
EPS = 1e-6

_OFF_CONV = 0
_OFF_POOL = 3 * BRANCH_W
_OFF_CQ = _OFF_POOL + BRANCH_W
_OFF_CK = _OFF_CQ + GQA_HEADS * HEAD_DIM
_OFF_CV = _OFF_CK + GQA_KV * HEAD_DIM
_OFF_DQ = _OFF_CV + GQA_KV * HEAD_DIM
_OFF_DK = _OFF_DQ + DIFF_HEADS * 2 * HEAD_DIM
_OFF_DV = _OFF_DK + DIFF_HEADS * 2 * HEAD_DIM
_OFF_GATE = _OFF_DV + DIFF_HEADS * DIFF_V
IN_W = _OFF_GATE + N_BRANCH * D_MODEL

LANES = 128
BF16_SUBLANES = 16
VMEM_LIMIT_BYTES = 56 * 1024 * 1024

HALO = BF16_SUBLANES
ONES_ROWS = BF16_SUBLANES
NEG_BIG = -1e30
MXU_DEPTH = 256
SCORE_TILE_ELEMS = 1024 * 2048


def _tiles(seq):
    tkv = min(1024, seq // 4)
    cols = SCORE_TILE_ELEMS // tkv
    return dict(
        ffn_tm=512,
        proj_tm=256,
        merge_tm=512,
        gqa_tq=min(cols // GQA_HEADS, seq),
        diff_tq=min(cols // 2, seq),
        tkv=tkv,
    )


def _resident(shape):
    zeros = (0,) * len(shape)
    return pl.BlockSpec(shape, lambda *_: zeros, pipeline_mode=pl.Buffered(1))


def _params(n_axes):
    return pltpu.CompilerParams(dimension_semantics=("arbitrary",) * n_axes,
                                vmem_limit_bytes=VMEM_LIMIT_BYTES)


def _rms(xf, gain):
    return xf * lax.rsqrt(jnp.mean(xf * xf, axis=-1, keepdims=True) + EPS) * gain


def _ffn_kernel(x_ref, g_ref, win_ref, wout_ref, o_ref):
    x = x_ref[...]
    u = _rms(x, g_ref[...]).astype(BF16)
    h = jnp.dot(u, win_ref[...], preferred_element_type=F32)
    a = h[:, :D_FF]
    b = h[:, D_FF:]
    act = (a * jax.nn.sigmoid(a) * b).astype(BF16)
    y = jnp.dot(act, wout_ref[...], preferred_element_type=F32)
    o_ref[...] = x + 0.5 * y


def _ffn(x2, gain, w_in, w_out, tm):
    T = x2.shape[0]
    return pl.pallas_call(
        _ffn_kernel,
        grid=(T // tm,),
        in_specs=[pl.BlockSpec((tm, D_MODEL), lambda i: (i, 0)),
                  _resident((1, D_MODEL)),
                  _resident((D_MODEL, 2 * D_FF)),
                  _resident((D_FF, D_MODEL))],
        out_specs=pl.BlockSpec((tm, D_MODEL), lambda i: (i, 0)),
        out_shape=jax.ShapeDtypeStruct((T, D_MODEL), F32),
        compiler_params=_params(1),
        name="ffn",
    )(x2, gain, w_in, w_out)


def _head_sum_squares(z, bd):
    return jnp.dot((z * z).astype(BF16), bd, preferred_element_type=F32)


def _head_norm_rope(z, ss, gain, cos, sin_a, sin_b, shift, scale):
    zn = z * lax.rsqrt(ss * (1.0 / HEAD_DIM) + EPS) * gain
    out = zn * cos + pltpu.roll(zn, LANES - shift, 1) * sin_a + pltpu.roll(zn, shift, 1) * sin_b
    if scale != 1.0:
        out = out * scale
    return out.astype(BF16)


def _proj_kernel(x_ref, g_ref, w_ref, bgate_ref, bd_ref, qn_ref, kn_ref, dqn_ref, dkn_ref,
                 acos_ref, asa_ref, asb_ref, pcos_ref, psa_ref, psb_ref,
                 conv_o, pool_o, q_o, k_o, vt_o, dq_o, dk_o, dvt_o, gate_o):
    u = _rms(x_ref[...], g_ref[...]).astype(BF16)

    def proj(lo, width):
        return jnp.dot(u, w_ref[:, lo:lo + width], preferred_element_type=F32)

    conv_o[...] = proj(_OFF_CONV, 3 * BRANCH_W).astype(BF16)
    pool_o[...] = proj(_OFF_POOL, BRANCH_W).astype(BF16)

    bd = bd_ref[...]
    axial = (acos_ref[...], asa_ref[...], asb_ref[...], HEAD_DIM // 4)
    partial = (pcos_ref[...], psa_ref[...], psb_ref[...], ROPE_DIM // 2)
    q_scale = HEAD_DIM ** -0.5 * math.log2(math.e)

    def heads(z, width, out_ref, gain_ref, tables, scale):
        gain = gain_ref[...]
        for j in range(0, width, 2 * LANES):
            ss = _head_sum_squares(z[:, j:j + 2 * LANES], bd)
            for lo in range(j, min(j + 2 * LANES, width), LANES):
                sl = slice(lo, lo + LANES)
                out_ref[:, sl] = _head_norm_rope(z[:, sl], ss[:, lo - j:lo - j + LANES], gain,
                                                 *tables, scale)

    def values_t(v, n_heads, ch, out_ref):
        vt = v.T.astype(BF16)
        ones = jnp.ones((ONES_ROWS, vt.shape[1]), BF16)
        rows = ch + ONES_ROWS
        for h in range(n_heads):
            out_ref[h * rows:h * rows + ch, :] = vt[h * ch:(h + 1) * ch, :]
            out_ref[h * rows + ch:(h + 1) * rows, :] = ones

    n_q, n_k = GQA_HEADS * HEAD_DIM, GQA_KV * HEAD_DIM
    heads(proj(_OFF_CQ, n_q), n_q, q_o, qn_ref, axial, q_scale)
    kv = proj(_OFF_CK, 2 * n_k)
    heads(kv, n_k, k_o, kn_ref, axial, 1.0)
    values_t(kv[:, n_k:], GQA_KV, HEAD_DIM, vt_o)
    n_d = DIFF_HEADS * 2 * HEAD_DIM
    heads(proj(_OFF_DQ, n_d), n_d, dq_o, dqn_ref, partial, q_scale)
    heads(proj(_OFF_DK, n_d), n_d, dk_o, dkn_ref, partial, 1.0)
    values_t(proj(_OFF_DV, DIFF_HEADS * DIFF_V), DIFF_HEADS, DIFF_V, dvt_o)

    for n in range(N_BRANCH):
        sl = slice(n * D_MODEL, (n + 1) * D_MODEL)
        g = proj(_OFF_GATE + n * D_MODEL, D_MODEL) + bgate_ref[:, sl]
        gate_o[:, sl] = jax.nn.sigmoid(g).astype(BF16)


def _proj(x2, lw, tabs, seq, tm):
    T = x2.shape[0]
    n_s = seq // tm
    tok = lambda w: pl.BlockSpec((tm, w), lambda i: (i, 0))
    tab = pl.BlockSpec((tm, LANES), lambda i: (i % n_s, 0))
    tok_t = lambda w: pl.BlockSpec((w, tm), lambda i: (0, i))
    outs = ((3 * BRANCH_W, False), (BRANCH_W, False), (GQA_HEADS * HEAD_DIM, False),
            (GQA_KV * HEAD_DIM, False), (GQA_KV * (HEAD_DIM + ONES_ROWS), True),
            (DIFF_HEADS * 2 * HEAD_DIM, False), (DIFF_HEADS * 2 * HEAD_DIM, False),
            (DIFF_HEADS * (DIFF_V + ONES_ROWS), True), (N_BRANCH * D_MODEL, False))
    return pl.pallas_call(
        _proj_kernel,
        grid=(T // tm,),
        in_specs=[tok(D_MODEL), _resident((1, D_MODEL)), _resident((D_MODEL, IN_W)),
                  _resident((1, N_BRANCH * D_MODEL)), _resident((2 * LANES, 2 * LANES)),
                  _resident((1, LANES)), _resident((1, LANES)), _resident((1, LANES)),
                  _resident((1, LANES))] + [tab] * 6,
        out_specs=[tok_t(w) if tr else tok(w) for w, tr in outs],
        out_shape=[jax.ShapeDtypeStruct((w, T) if tr else (T, w), BF16) for w, tr in outs],
        compiler_params=_params(1),
        name="proj",
    )(x2, lw["mix_norm"], lw["w_in"], lw["b_gate"], lw["bd"], lw["q_norm"], lw["k_norm"],
      lw["dq_norm"], lw["dk_norm"], *tabs)


def _flash_cols(qst_ref, k_ref, vt_ref, m_ref, acc_ref, s_ref, smax_ref, alpha_ref, p_ref,
                *, tkv, n_kv, pv_groups):
    m_ref[...] = jnp.full(m_ref.shape, NEG_BIG, F32)
    acc_ref[...] = jnp.zeros(acc_ref.shape, F32)

    ck = min(MXU_DEPTH, tkv)
    n_chunks = tkv // ck

    def kv_start(u, c):
        base = u * tkv if isinstance(u, int) else pl.multiple_of(u * tkv, tkv)
        return base + c * ck

    def slot(scores_of=None, softmax=False, pv_of=None):
        if pv_of is not None:
            alpha = alpha_ref[...]
        if softmax:
            m_prev = m_ref[...]
            m_new = jnp.maximum(m_prev, smax_ref[...])
            alpha_ref[...] = jnp.exp2(m_prev - m_new)
            m_ref[...] = m_new
        col_max = None
        for c in range(n_chunks):
            rows = slice(c * ck, (c + 1) * ck)
            if pv_of is not None:
                kv = pl.ds(kv_start(pv_of, c), ck)
                for row0, cols in pv_groups:
                    vt = vt_ref[row0:row0 + acc_ref.shape[0], kv]
                    part = jnp.dot(vt, p_ref[rows, cols], preferred_element_type=F32)
                    if c == 0:
                        acc_ref[:, cols] = alpha[:, cols] * acc_ref[:, cols] + part
                    else:
                        acc_ref[:, cols] += part
            if softmax:
                p_ref[rows, :] = jnp.exp2(s_ref[rows, :] - m_new).astype(BF16)
            if scores_of is not None:
                k = k_ref[0, pl.ds(kv_start(scores_of, c), ck), :]
                s = jnp.dot(k, qst_ref[...], preferred_element_type=F32)
                s_ref[rows, :] = s
                part = jnp.max(s, axis=0, keepdims=True)
                col_max = part if col_max is None else jnp.maximum(col_max, part)
        if scores_of is not None:
            smax_ref[...] = col_max

    slot(scores_of=0)
    slot(scores_of=1, softmax=True)

    def steady(u, carry):
        slot(scores_of=u, softmax=True, pv_of=u - 2)
        return carry

    lax.fori_loop(2, n_kv, steady, 0, unroll=2 if n_kv - 2 >= 4 else 1)
    slot(softmax=True, pv_of=n_kv - 2)
    slot(pv_of=n_kv - 1)


def _stack_queries(x, qst_ref, lo_cols, hi_cols):
    xt = x.astype(F32).T
    top = lax.broadcasted_iota(jnp.int32, xt.shape, 0) < HEAD_DIM
    zero = jnp.zeros_like(xt)
    qst_ref[:, lo_cols] = jnp.where(top, xt, zero).astype(BF16)
    qst_ref[:, hi_cols] = jnp.where(top, zero, xt).astype(BF16)


def _attn_scratch(R, tkv, acc_rows):
    return [pltpu.VMEM((LANES, R), BF16), pltpu.VMEM((1, R), F32),
            pltpu.VMEM((acc_rows, R), F32), pltpu.VMEM((tkv, R), F32),
            pltpu.VMEM((1, R), F32), pltpu.VMEM((1, R), F32),
            pltpu.VMEM((tkv, R), BF16)]


def _gqa_kernel(q_ref, k_ref, vt_ref, o_ref, qst_ref, m_ref, acc_ref, *pipe, tq, tkv, n_kv):
    n_blk = GQA_HEADS // GQA_KV
    col = lambda h: slice(h * tq, (h + 1) * tq)
    for j in range(n_blk):
        _stack_queries(q_ref[0, :, j * LANES:(j + 1) * LANES], qst_ref, col(j), col(n_blk + j))
    half = n_blk * tq
    _flash_cols(qst_ref, k_ref, vt_ref, m_ref, acc_ref, *pipe, tkv=tkv, n_kv=n_kv,
                pv_groups=((0, slice(0, half)), (HEAD_DIM + ONES_ROWS, slice(half, 2 * half))))
    out, den = slice(0, HEAD_DIM), slice(HEAD_DIM, HEAD_DIM + 1)
    for j in range(n_blk):
        c0, c1 = col(j), col(n_blk + j)
        top = acc_ref[out, c0] / acc_ref[den, c0]
        bot = acc_ref[out, c1] / acc_ref[den, c1]
        o_ref[0, :, j * LANES:(j + 1) * LANES] = (
            jnp.concatenate([top, bot], axis=0).T.astype(BF16))


def _gqa(q, k, vt, t):
    B, S, W = q.shape
    tq, tkv = t["gqa_tq"], t["tkv"]
    kern = functools.partial(_gqa_kernel, tq=tq, tkv=tkv, n_kv=S // tkv)
    return pl.pallas_call(
        kern,
        grid=(B, S // tq),
        in_specs=[pl.BlockSpec((1, tq, W), lambda b, i: (b, i, 0)),
                  pl.BlockSpec((1, S, LANES), lambda b, i: (b, 0, 0)),
                  pl.BlockSpec((GQA_KV * (HEAD_DIM + ONES_ROWS), S), lambda b, i: (0, b))],
        out_specs=pl.BlockSpec((1, tq, W), lambda b, i: (b, i, 0)),
        out_shape=jax.ShapeDtypeStruct((B, S, W), BF16),
        scratch_shapes=_attn_scratch(GQA_HEADS * tq, tkv, HEAD_DIM + ONES_ROWS),
        compiler_params=_params(2),
        name="gqa",
    )(q, k, vt)


def _diff_kernel(lam_ref, on_ref, q_ref, k_ref, vt_ref, o_ref, qst_ref, m_ref, acc_ref,
                 *pipe, tq, tkv, n_kv, lambda_init):
    c0, c1 = slice(0, tq), slice(tq, 2 * tq)
    _stack_queries(q_ref[0], qst_ref, c0, c1)
    _flash_cols(qst_ref, k_ref, vt_ref, m_ref, acc_ref, *pipe, tkv=tkv, n_kv=n_kv,
                pv_groups=((0, slice(0, 2 * tq)),))
    lv = lam_ref[...]
    lam = (jnp.exp(jnp.sum(lv[0:1] * lv[1:2], axis=1, keepdims=True))
           - jnp.exp(jnp.sum(lv[2:3] * lv[3:4], axis=1, keepdims=True)) + lambda_init)
    out, den = slice(0, DIFF_V), slice(DIFF_V, DIFF_V + 1)
    ot = (acc_ref[out, c0] / acc_ref[den, c0]
          - lam * (acc_ref[out, c1] / acc_ref[den, c1]))
    o_ref[0] = (_rms(ot.T, on_ref[...]) * (1.0 - lambda_init)).astype(BF16)


def _diff(q, k, vt, lam, out_norm, lambda_init, t):
    B, S, W = q.shape
    tq, tkv = t["diff_tq"], t["tkv"]
    kern = functools.partial(_diff_kernel, tq=tq, tkv=tkv, n_kv=S // tkv,
                             lambda_init=lambda_init)
    return pl.pallas_call(
        kern,
        grid=(B, DIFF_HEADS, S // tq),
        in_specs=[pl.BlockSpec((4, HEAD_DIM), lambda b, h, i: (0, 0)),
                  pl.BlockSpec((1, DIFF_V), lambda b, h, i: (0, 0)),
                  pl.BlockSpec((1, tq, LANES), lambda b, h, i: (b, i, h)),
                  pl.BlockSpec((1, S, LANES), lambda b, h, i: (b, 0, h)),
                  pl.BlockSpec((DIFF_V + ONES_ROWS, S), lambda b, h, i: (h, b))],
        out_specs=pl.BlockSpec((1, tq, LANES), lambda b, h, i: (b, i, h)),
        out_shape=jax.ShapeDtypeStruct((B, S, W), BF16),
        scratch_shapes=_attn_scratch(2 * tq, tkv, DIFF_V + ONES_ROWS),
        compiler_params=_params(3),
        name="diff",
    )(lam, out_norm, q, k, vt)


def _merge_kernel(x_ref, conv_ref, convp_ref, convn_ref, pool_ref, poolp_ref, pooln_ref,
                  og_ref, od_ref, gate_ref, cw_ref, pw_ref, ps_ref, wb_ref, wo_ref, o_ref,
                  cbuf, pbuf, *, tm, n_s, seq):
    si = pl.program_id(0) % n_s
    keep_prev = (si > 0).astype(F32)
    keep_next = (si < n_s - 1).astype(F32)
    W = BRANCH_W

    def gated(ref):
        blk = ref[...].astype(F32)
        return blk[:, 2 * W:3 * W] * blk[:, 0:W]
    cbuf[0:HALO, :] = gated(convp_ref) * keep_prev
    cbuf[HALO:HALO + tm, :] = gated(conv_ref)
    cbuf[HALO + tm:2 * HALO + tm, :] = gated(convn_ref) * keep_next
    cw = cw_ref[...]
    conv = (cw[0:1] * cbuf[HALO - 1:HALO - 1 + tm, :] + cw[1:2] * cbuf[HALO:HALO + tm, :]
            + cw[2:3] * cbuf[HALO + 1:HALO + 1 + tm, :])
    br_a = (conv_ref[:, W:2 * W].astype(F32) * conv).astype(BF16)

    pbuf[0:HALO, :] = poolp_ref[...].astype(F32) * keep_prev
    pbuf[HALO:HALO + tm, :] = pool_ref[...].astype(F32)
    pbuf[HALO + tm:2 * HALO + tm, :] = pooln_ref[...].astype(F32) * keep_next
    pos = si * tm + lax.broadcasted_iota(jnp.int32, (tm, 1), 0)
    pooled = []
    for gi, win in enumerate(POOL_WINDOWS):
        lanes = slice(gi * POOL_GROUP, (gi + 1) * POOL_GROUP)
        half = win // 2
        tot = pbuf[HALO - half:HALO - half + tm, lanes]
        for off in range(-half + 1, half):
            tot = tot + pbuf[HALO + off:HALO + off + tm, lanes]
        cnt = (jnp.clip(pos - half + win, 0, seq) - jnp.clip(pos - half, 0, seq)).astype(F32)
        mean_diff = (tot / cnt - pbuf[HALO:HALO + tm, lanes]).astype(BF16)
        pooled.append(jnp.dot(mean_diff, pw_ref[gi], preferred_element_type=F32))
    br_b = (jnp.concatenate(pooled, axis=1) * ps_ref[...]).astype(BF16)

    merged = None
    for n, br in enumerate((br_a, br_b, og_ref[...], od_ref[...])):
        y = jnp.dot(br, wb_ref[n], preferred_element_type=F32)
        y = gate_ref[:, n * D_MODEL:(n + 1) * D_MODEL].astype(F32) * y
        merged = y if merged is None else merged + y
    o_ref[...] = x_ref[...] + jnp.dot(merged.astype(BF16), wo_ref[...],
                                      preferred_element_type=F32)


def _merge(x2, conv, pool, og, od, gates, lw, seq, tm):
    T = x2.shape[0]
    n_s = seq // tm
    per = tm // HALO
    last = T // HALO - 1
    tok = lambda w: pl.BlockSpec((tm, w), lambda i: (i, 0))
    prev = lambda w: pl.BlockSpec((HALO, w), lambda i: (jnp.maximum(i * per - 1, 0), 0))
    nxt = lambda w: pl.BlockSpec((HALO, w), lambda i: (jnp.minimum((i + 1) * per, last), 0))
    kern = functools.partial(_merge_kernel, tm=tm, n_s=n_s, seq=seq)
    return pl.pallas_call(
        kern,
        grid=(T // tm,),
        in_specs=[tok(D_MODEL),
                  tok(3 * BRANCH_W), prev(3 * BRANCH_W), nxt(3 * BRANCH_W),
                  tok(BRANCH_W), prev(BRANCH_W), nxt(BRANCH_W),
                  tok(BRANCH_W), tok(BRANCH_W), tok(N_BRANCH * D_MODEL),
                  _resident((3, BRANCH_W)), _resident((4, POOL_GROUP, POOL_GROUP)),
                  _resident((1, BRANCH_W)), _resident((N_BRANCH, BRANCH_W, D_MODEL)),
                  _resident((D_MODEL, D_MODEL))],
        out_specs=tok(D_MODEL),
        out_shape=jax.ShapeDtypeStruct((T, D_MODEL), F32),
        scratch_shapes=[pltpu.VMEM((tm + 2 * HALO, BRANCH_W), F32),
                        pltpu.VMEM((tm + 2 * HALO, BRANCH_W), F32)],
        compiler_params=_params(1),
        name="merge",
    )(x2, conv, conv, conv, pool, pool, pool, og, od, gates,
      lw["conv_w"], lw["pool_w"], lw["pool_scale"], lw["w_branch"], lw["w_out"])


def _gqa_lane_perm():
    n_blk = GQA_HEADS // GQA_KV
    perm = np.empty(GQA_HEADS * HEAD_DIM, np.int32)
    for j in range(n_blk):
        for g in range(GQA_KV):
            dst = j * LANES + g * HEAD_DIM
            src = (g * n_blk + j) * HEAD_DIM
            perm[dst:dst + HEAD_DIM] = np.arange(src, src + HEAD_DIM)
    return perm


def _rope_tables(seq):
    d = np.arange(LANES) % HEAD_DIM
    t = jnp.arange(seq)

    def tables(pos, n, freq_idx, first_half, roped):
        freq = jnp.exp(-math.log(n[1]) * jnp.asarray(freq_idx, F32) * (2.0 / n[0]))
        ang = pos.astype(F32) * freq[None, :]
        c, s = jnp.cos(ang), jnp.sin(ang)
        roped, first_half = jnp.asarray(roped)[None, :], jnp.asarray(first_half)[None, :]
        return (jnp.where(roped, c, 1.0), jnp.where(roped & first_half, -s, 0.0),
                jnp.where(roped & ~first_half, s, 0.0))

    half_dim = HEAD_DIM // 2
    r = d % half_dim
    pos_axial = jnp.where(jnp.asarray(d < half_dim)[None, :], (t // GRID_W)[:, None],
                          (t % GRID_W)[:, None])
    axial = tables(pos_axial, (half_dim, AXIAL_THETA), r % (half_dim // 2), r < half_dim // 2,
                   np.ones(LANES, bool))
    partial = tables(jnp.broadcast_to(t[:, None], (seq, LANES)), (ROPE_DIM, ROPE_THETA),
                     d % (ROPE_DIM // 2), d < ROPE_DIM // 2, d < ROPE_DIM)
    return axial + partial


def _layer_weights(i, ffn1_norm, ffn1_w_in, ffn1_w_out, mix_norm, w_in, b_gate, conv_w, pool_w,
                   pool_scale, attn_q_norm, attn_k_norm, diff_q_norm, diff_k_norm, diff_lambda,
                   diff_out_norm, w_branch, w_out, ffn2_norm, ffn2_w_in, ffn2_w_out):
    perm = _gqa_lane_perm()
    wi = w_in[i]
    wi = wi.at[:, _OFF_CQ:_OFF_CK].set(wi[:, _OFF_CQ:_OFF_CK][:, perm])
    wb = w_branch[i]
    wb = wb.at[2].set(wb[2][perm, :])
    two = lambda g: jnp.concatenate([g, g]).reshape(1, LANES).astype(F32)
    blk = np.arange(2 * LANES) // HEAD_DIM
    return dict(
        ffn1=(ffn1_norm[i].reshape(1, D_MODEL), ffn1_w_in[i].astype(BF16),
              ffn1_w_out[i].astype(BF16)),
        ffn2=(ffn2_norm[i].reshape(1, D_MODEL), ffn2_w_in[i].astype(BF16),
              ffn2_w_out[i].astype(BF16)),
        mix_norm=mix_norm[i].reshape(1, D_MODEL),
        w_in=wi.astype(BF16),
        b_gate=b_gate[i].reshape(1, N_BRANCH * D_MODEL),
        bd=jnp.asarray(blk[:, None] == blk[None, :], BF16),
        q_norm=two(attn_q_norm[i]), k_norm=two(attn_k_norm[i]),
        dq_norm=two(diff_q_norm[i]), dk_norm=two(diff_k_norm[i]),
        diff_lambda=diff_lambda[i].astype(F32),
        diff_out_norm=diff_out_norm[i].reshape(1, DIFF_V),
        lambda_init=0.8 - 0.6 * math.exp(-0.3 * i),
        conv_w=conv_w[i], pool_w=pool_w[i].astype(BF16),
        pool_scale=pool_scale[i].reshape(1, BRANCH_W),
        w_branch=wb.astype(BF16), w_out=w_out[i].astype(BF16),
    )


def _trunk(x, layers):
    B, S, _ = x.shape
    T = B * S
    t = _tiles(S)
    tabs = _rope_tables(S)
    x2 = x.reshape(T, D_MODEL)
    for lw in layers:
        x2 = _ffn(x2, *lw["ffn1"], t["ffn_tm"])
        conv, pool, q, k, vt, dq, dk, dvt, gates = _proj(x2, lw, tabs, S, t["proj_tm"])
        seq3 = lambda a: a.reshape(B, S, a.shape[-1])
        og = _gqa(seq3(q), seq3(k), vt, t)
        od = _diff(seq3(dq), seq3(dk), dvt, lw["diff_lambda"], lw["diff_out_norm"],
                   lw["lambda_init"], t)
        x2 = _merge(x2, conv, pool, og.reshape(T, -1), od.reshape(T, -1), gates, lw, S,
                    t["merge_tm"])
        x2 = _ffn(x2, *lw["ffn2"], t["ffn_tm"])
    return x2.reshape(B, S, D_MODEL)


def kernel(x_prompt, x_sample, ffn1_norm, ffn1_w_in, ffn1_w_out, mix_norm, w_in, b_gate, conv_w, pool_w, pool_scale, attn_q_norm, attn_k_norm, diff_q_norm, diff_k_norm, diff_lambda, diff_out_norm, w_branch, w_out, ffn2_norm, ffn2_w_in, ffn2_w_out):
    weights = (ffn1_norm, ffn1_w_in, ffn1_w_out, mix_norm, w_in, b_gate, conv_w, pool_w,
               pool_scale, attn_q_norm, attn_k_norm, diff_q_norm, diff_k_norm, diff_lambda,
               diff_out_norm, w_branch, w_out, ffn2_norm, ffn2_w_in, ffn2_w_out)
    layers = [_layer_weights(i, *weights) for i in range(DEPTH)]
    return (_trunk(x_prompt, layers), _trunk(x_sample, layers))
```

```python
import functools
import math

import jax
import jax.numpy as jnp
import numpy as np
from jax import lax
from jax.experimental import pallas as pl
from jax.experimental.pallas import tpu as pltpu

F32 = jnp.float32
BF16 = jnp.bfloat16

D_MODEL = 1024
DEPTH = 2
GRID_W = 64
HEAD_DIM = 64
D_FF = 2816
N_BRANCH = 4
BRANCH_W = 512
POOL_WINDOWS = (2, 4, 8, 16)
POOL_GROUP = BRANCH_W // 4
GQA_HEADS = 8
GQA_KV = 2
DIFF_HEADS = 4
DIFF_V = 2 * HEAD_DIM
AXIAL_THETA = 10000.0
ROPE_THETA = 500000.0
ROPE_DIM = HEAD_DIM // 4
EPS = 1e-6

_OFF_CONV = 0
_OFF_POOL = 3 * BRANCH_W
_OFF_CQ = _OFF_POOL + BRANCH_W
_OFF_CK = _OFF_CQ + GQA_HEADS * HEAD_DIM
_OFF_CV = _OFF_CK + GQA_KV * HEAD_DIM
_OFF_DQ = _OFF_CV + GQA_KV * HEAD_DIM
_OFF_DK = _OFF_DQ + DIFF_HEADS * 2 * HEAD_DIM
_OFF_DV = _OFF_DK + DIFF_HEADS * 2 * HEAD_DIM
_OFF_GATE = _OFF_DV + DIFF_HEADS * DIFF_V
IN_W = _OFF_GATE + N_BRANCH * D_MODEL

LANES = 128
BF16_SUBLANES = 16
VMEM_LIMIT_BYTES = 56 * 1024 * 1024

HALO = BF16_SUBLANES
ONES_ROWS = BF16_SUBLANES
NEG_BIG = -1e30
MXU_DEPTH = 256
SCORE_TILE_ELEMS = 1024 * 2048


def _tiles(seq):
    tkv = min(1024, seq // 4)
    cols = SCORE_TILE_ELEMS // tkv
    return dict(
        ffn_tm=512,
        proj_tm=256,
        merge_tm=512,
        gqa_tq=min(cols // GQA_HEADS, seq),
        diff_tq=min(cols // 2, seq),
        tkv=tkv,
    )


def _resident(shape):
    zeros = (0,) * len(shape)
    return pl.BlockSpec(shape, lambda *_: zeros, pipeline_mode=pl.Buffered(1))


def _params(n_axes):
    return pltpu.CompilerParams(dimension_semantics=("arbitrary",) * n_axes,
                                vmem_limit_bytes=VMEM_LIMIT_BYTES)


def _rms(xf, gain):
    return xf * lax.rsqrt(jnp.mean(xf * xf, axis=-1, keepdims=True) + EPS) * gain


def _ffn_kernel(x_ref, g_ref, win_ref, wout_ref, o_ref):
    x = x_ref[...]
    u = _rms(x, g_ref[...]).astype(BF16)
    h = jnp.dot(u, win_ref[...], preferred_element_type=F32)
    a = h[:, :D_FF]
    b = h[:, D_FF:]
    act = (a * jax.nn.sigmoid(a) * b).astype(BF16)
    y = jnp.dot(act, wout_ref[...], preferred_element_type=F32)
    o_ref[...] = x + 0.5 * y


def _ffn(x2, gain, w_in, w_out, tm):
    T = x2.shape[0]
    return pl.pallas_call(
        _ffn_kernel,
        grid=(T // tm,),
        in_specs=[pl.BlockSpec((tm, D_MODEL), lambda i: (i, 0)),
                  _resident((1, D_MODEL)),
                  _resident((D_MODEL, 2 * D_FF)),
                  _resident((D_FF, D_MODEL))],
        out_specs=pl.BlockSpec((tm, D_MODEL), lambda i: (i, 0)),
        out_shape=jax.ShapeDtypeStruct((T, D_MODEL), F32),
        compiler_params=_params(1),
        name="ffn",
    )(x2, gain, w_in, w_out)


def _head_sum_squares(z, bd):
    return jnp.dot((z * z).astype(BF16), bd, preferred_element_type=F32)


def _head_norm_rope(z, ss, gain, cos, sin_a, sin_b, shift, scale):
    zn = z * lax.rsqrt(ss * (1.0 / HEAD_DIM) + EPS) * gain
    out = zn * cos + pltpu.roll(zn, LANES - shift, 1) * sin_a + pltpu.roll(zn, shift, 1) * sin_b
    if scale != 1.0:
        out = out * scale
    return out.astype(BF16)


def _proj_kernel(x_ref, g_ref, w_ref, bgate_ref, bd_ref, qn_ref, kn_ref, dqn_ref, dkn_ref,
                 acos_ref, asa_ref, asb_ref, pcos_ref, psa_ref, psb_ref,
                 conv_o, pool_o, q_o, k_o, vt_o, dq_o, dk_o, dvt_o, gate_o):
    u = _rms(x_ref[...], g_ref[...]).astype(BF16)

    def proj(lo, width):
        return jnp.dot(u, w_ref[:, lo:lo + width], preferred_element_type=F32)

    conv_o[...] = proj(_OFF_CONV, 3 * BRANCH_W).astype(BF16)
    pool_o[...] = proj(_OFF_POOL, BRANCH_W).astype(BF16)

    bd = bd_ref[...]
    axial = (acos_ref[...], asa_ref[...], asb_ref[...], HEAD_DIM // 4)
    partial = (pcos_ref[...], psa_ref[...], psb_ref[...], ROPE_DIM // 2)
    q_scale = HEAD_DIM ** -0.5 * math.log2(math.e)

    def heads(z, width, out_ref, gain_ref, tables, scale):
        gain = gain_ref[...]
        for j in range(0, width, 2 * LANES):
            ss = _head_sum_squares(z[:, j:j + 2 * LANES], bd)
            for lo in range(j, min(j + 2 * LANES, width), LANES):
                sl = slice(lo, lo + LANES)
                out_ref[:, sl] = _head_norm_rope(z[:, sl], ss[:, lo - j:lo - j + LANES], gain,
                                                 *tables, scale)

    def values_t(v, n_heads, ch, out_ref):
        vt = v.T.astype(BF16)
        ones = jnp.ones((ONES_ROWS, vt.shape[1]), BF16)
        rows = ch + ONES_ROWS
        for h in range(n_heads):
            out_ref[h * rows:h * rows + ch, :] = vt[h * ch:(h + 1) * ch, :]
            out_ref[h * rows + ch:(h + 1) * rows, :] = ones

    n_q, n_k = GQA_HEADS * HEAD_DIM, GQA_KV * HEAD_DIM
    heads(proj(_OFF_CQ, n_q), n_q, q_o, qn_ref, axial, q_scale)
    kv = proj(_OFF_CK, 2 * n_k)
    heads(kv, n_k, k_o, kn_ref, axial, 1.0)
    values_t(kv[:, n_k:], GQA_KV, HEAD_DIM, vt_o)
    n_d = DIFF_HEADS * 2 * HEAD_DIM
    heads(proj(_OFF_DQ, n_d), n_d, dq_o, dqn_ref, partial, q_scale)
    heads(proj(_OFF_DK, n_d), n_d, dk_o, dkn_ref, partial, 1.0)
    values_t(proj(_OFF_DV, DIFF_HEADS * DIFF_V), DIFF_HEADS, DIFF_V, dvt_o)

    for n in range(N_BRANCH):
        sl = slice(n * D_MODEL, (n + 1) * D_MODEL)
        g = proj(_OFF_GATE + n * D_MODEL, D_MODEL) + bgate_ref[:, sl]
        gate_o[:, sl] = jax.nn.sigmoid(g).astype(BF16)


def _proj(x2, lw, tabs, seq, tm):
    T = x2.shape[0]
    n_s = seq // tm
    tok = lambda w: pl.BlockSpec((tm, w), lambda i: (i, 0))
    tab = pl.BlockSpec((tm, LANES), lambda i: (i % n_s, 0))
    tok_t = lambda w: pl.BlockSpec((w, tm), lambda i: (0, i))
    outs = ((3 * BRANCH_W, False), (BRANCH_W, False), (GQA_HEADS * HEAD_DIM, False),
            (GQA_KV * HEAD_DIM, False), (GQA_KV * (HEAD_DIM + ONES_ROWS), True),
            (DIFF_HEADS * 2 * HEAD_DIM, False), (DIFF_HEADS * 2 * HEAD_DIM, False),
            (DIFF_HEADS * (DIFF_V + ONES_ROWS), True), (N_BRANCH * D_MODEL, False))
    return pl.pallas_call(
        _proj_kernel,
        grid=(T // tm,),
        in_specs=[tok(D_MODEL), _resident((1, D_MODEL)), _resident((D_MODEL, IN_W)),
                  _resident((1, N_BRANCH * D_MODEL)), _resident((2 * LANES, 2 * LANES)),
                  _resident((1, LANES)), _resident((1, LANES)), _resident((1, LANES)),
                  _resident((1, LANES))] + [tab] * 6,
        out_specs=[tok_t(w) if tr else tok(w) for w, tr in outs],
        out_shape=[jax.ShapeDtypeStruct((w, T) if tr else (T, w), BF16) for w, tr in outs],
        compiler_params=_params(1),
        name="proj",
    )(x2, lw["mix_norm"], lw["w_in"], lw["b_gate"], lw["bd"], lw["q_norm"], lw["k_norm"],
      lw["dq_norm"], lw["dk_norm"], *tabs)


def _flash_cols(qst_ref, k_ref, vt_ref, m_ref, acc_ref, s_ref, smax_ref, alpha_ref, p_ref,
                *, tkv, n_kv, pv_groups):
    m_ref[...] = jnp.full(m_ref.shape, NEG_BIG, F32)
    acc_ref[...] = jnp.zeros(acc_ref.shape, F32)

    ck = min(MXU_DEPTH, tkv)
    n_chunks = tkv // ck

    def kv_start(u, c):
        base = u * tkv if isinstance(u, int) else pl.multiple_of(u * tkv, tkv)
        return base + c * ck

    def slot(scores_of=None, softmax=False, pv_of=None):
        if pv_of is not None:
            alpha = alpha_ref[...]
        if softmax:
            m_prev = m_ref[...]
            m_new = jnp.maximum(m_prev, smax_ref[...])
            alpha_ref[...] = jnp.exp2(m_prev - m_new)
            m_ref[...] = m_new
        col_max = None
        for c in range(n_chunks):
            rows = slice(c * ck, (c + 1) * ck)
            if pv_of is not None:
                kv = pl.ds(kv_start(pv_of, c), ck)
                for row0, cols in pv_groups:
                    vt = vt_ref[row0:row0 + acc_ref.shape[0], kv]
                    part = jnp.dot(vt, p_ref[rows, cols], preferred_element_type=F32)
                    if c == 0:
                        acc_ref[:, cols] = alpha[:, cols] * acc_ref[:, cols] + part
                    else:
                        acc_ref[:, cols] += part
            if softmax:
                p_ref[rows, :] = jnp.exp2(s_ref[rows, :] - m_new).astype(BF16)
            if scores_of is not None:
                k = k_ref[0, pl.ds(kv_start(scores_of, c), ck), :]
                s = jnp.dot(k, qst_ref[...], preferred_element_type=F32)
                s_ref[rows, :] = s
                part = jnp.max(s, axis=0, keepdims=True)
                col_max = part if col_max is None else jnp.maximum(col_max, part)
        if scores_of is not None:
            smax_ref[...] = col_max

    slot(scores_of=0)
    slot(scores_of=1, softmax=True)

    def steady(u, carry):
        slot(scores_of=u, softmax=True, pv_of=u - 2)
        return carry

    lax.fori_loop(2, n_kv, steady, 0, unroll=2 if n_kv - 2 >= 4 else 1)
    slot(softmax=True, pv_of=n_kv - 2)
    slot(pv_of=n_kv - 1)


def _stack_queries(x, qst_ref, cols, k_halves):
    xt = x.astype(F32).T
    zero = jnp.zeros((HEAD_DIM, xt.shape[1]), F32)
    for part, c, k_half in zip((xt[:HEAD_DIM], xt[HEAD_DIM:]), cols, k_halves):
        block = [part, zero] if k_half == 0 else [zero, part]
        qst_ref[:, c] = jnp.concatenate(block, axis=0).astype(BF16)


def _attn_scratch(R, tkv, acc_rows):
    return [pltpu.VMEM((LANES, R), BF16), pltpu.VMEM((1, R), F32),
            pltpu.VMEM((acc_rows, R), F32), pltpu.VMEM((tkv, R), F32),
            pltpu.VMEM((1, R), F32), pltpu.VMEM((1, R), F32),
            pltpu.VMEM((tkv, R), BF16)]


def _gqa_kernel(q_ref, k_ref, vt_ref, o_ref, qst_ref, m_ref, acc_ref, *pipe, tq, tkv, n_kv):
    per_kv = GQA_HEADS // GQA_KV
    col = lambda h: slice(h * tq, (h + 1) * tq)
    for j in range(GQA_HEADS // 2):
        heads = (2 * j, 2 * j + 1)
        _stack_queries(q_ref[0, :, j * LANES:(j + 1) * LANES], qst_ref,
                       [col(h) for h in heads], [h // per_kv for h in heads])
    half = per_kv * tq
    _flash_cols(qst_ref, k_ref, vt_ref, m_ref, acc_ref, *pipe, tkv=tkv, n_kv=n_kv,
                pv_groups=((0, slice(0, half)), (HEAD_DIM + ONES_ROWS, slice(half, 2 * half))))
    out, den = slice(0, HEAD_DIM), slice(HEAD_DIM, HEAD_DIM + 1)
    for j in range(GQA_HEADS // 2):
        c0, c1 = col(2 * j), col(2 * j + 1)
        top = acc_ref[out, c0] / acc_ref[den, c0]
        bot = acc_ref[out, c1] / acc_ref[den, c1]
        o_ref[0, :, j * LANES:(j + 1) * LANES] = (
            jnp.concatenate([top, bot], axis=0).T.astype(BF16))


def _gqa(q, k, vt, t):
    B, S, W = q.shape
    tq, tkv = t["gqa_tq"], t["tkv"]
    kern = functools.partial(_gqa_kernel, tq=tq, tkv=tkv, n_kv=S // tkv)
    return pl.pallas_call(
        kern,
        grid=(B, S // tq),
        in_specs=[pl.BlockSpec((1, tq, W), lambda b, i: (b, i, 0)),
                  pl.BlockSpec((1, S, LANES), lambda b, i: (b, 0, 0)),
                  pl.BlockSpec((GQA_KV * (HEAD_DIM + ONES_ROWS), S), lambda b, i: (0, b))],
        out_specs=pl.BlockSpec((1, tq, W), lambda b, i: (b, i, 0)),
        out_shape=jax.ShapeDtypeStruct((B, S, W), BF16),
        scratch_shapes=_attn_scratch(GQA_HEADS * tq, tkv, HEAD_DIM + ONES_ROWS),
        compiler_params=_params(2),
        name="gqa",
    )(q, k, vt)


def _diff_kernel(lam_ref, on_ref, q_ref, k_ref, vt_ref, o_ref, qst_ref, m_ref, acc_ref,
                 *pipe, tq, tkv, n_kv, lambda_init):
    c0, c1 = slice(0, tq), slice(tq, 2 * tq)
    _stack_queries(q_ref[0], qst_ref, (c0, c1), (0, 1))
    _flash_cols(qst_ref, k_ref, vt_ref, m_ref, acc_ref, *pipe, tkv=tkv, n_kv=n_kv,
                pv_groups=((0, slice(0, 2 * tq)),))
    lv = lam_ref[...]
    lam = (jnp.exp(jnp.sum(lv[0:1] * lv[1:2], axis=1, keepdims=True))
           - jnp.exp(jnp.sum(lv[2:3] * lv[3:4], axis=1, keepdims=True)) + lambda_init)
    out, den = slice(0, DIFF_V), slice(DIFF_V, DIFF_V + 1)
    ot = (acc_ref[out, c0] / acc_ref[den, c0]
          - lam * (acc_ref[out, c1] / acc_ref[den, c1]))
    o_ref[0] = (_rms(ot.T, on_ref[...]) * (1.0 - lambda_init)).astype(BF16)


def _diff(q, k, vt, lam, out_norm, lambda_init, t):
    B, S, W = q.shape
    tq, tkv = t["diff_tq"], t["tkv"]
    kern = functools.partial(_diff_kernel, tq=tq, tkv=tkv, n_kv=S // tkv,
                             lambda_init=lambda_init)
    return pl.pallas_call(
        kern,
        grid=(B, DIFF_HEADS, S // tq),
        in_specs=[pl.BlockSpec((4, HEAD_DIM), lambda b, h, i: (0, 0)),
                  pl.BlockSpec((1, DIFF_V), lambda b, h, i: (0, 0)),
                  pl.BlockSpec((1, tq, LANES), lambda b, h, i: (b, i, h)),
                  pl.BlockSpec((1, S, LANES), lambda b, h, i: (b, 0, h)),
                  pl.BlockSpec((DIFF_V + ONES_ROWS, S), lambda b, h, i: (h, b))],
        out_specs=pl.BlockSpec((1, tq, LANES), lambda b, h, i: (b, i, h)),
        out_shape=jax.ShapeDtypeStruct((B, S, W), BF16),
        scratch_shapes=_attn_scratch(2 * tq, tkv, DIFF_V + ONES_ROWS),
        compiler_params=_params(3),
        name="diff",
    )(lam, out_norm, q, k, vt)


def _merge_kernel(x_ref, conv_ref, convp_ref, convn_ref, pool_ref, poolp_ref, pooln_ref,
                  og_ref, od_ref, gate_ref, cw_ref, pw_ref, ps_ref, wb_ref, wo_ref, o_ref,
                  cbuf, pbuf, *, tm, n_s, seq):
    si = pl.program_id(0) % n_s
    keep_prev = (si > 0).astype(F32)
    keep_next = (si < n_s - 1).astype(F32)
    W = BRANCH_W

    def gated(ref):
        blk = ref[...].astype(F32)
        return blk[:, 2 * W:3 * W] * blk[:, 0:W]
    cbuf[0:HALO, :] = gated(convp_ref) * keep_prev
    cbuf[HALO:HALO + tm, :] = gated(conv_ref)
    cbuf[HALO + tm:2 * HALO + tm, :] = gated(convn_ref) * keep_next
    cw = cw_ref[...]
    conv = (cw[0:1] * cbuf[HALO - 1:HALO - 1 + tm, :] + cw[1:2] * cbuf[HALO:HALO + tm, :]
            + cw[2:3] * cbuf[HALO + 1:HALO + 1 + tm, :])
    br_a = (conv_ref[:, W:2 * W].astype(F32) * conv).astype(BF16)

    pbuf[0:HALO, :] = poolp_ref[...].astype(F32) * keep_prev
    pbuf[HALO:HALO + tm, :] = pool_ref[...].astype(F32)
    pbuf[HALO + tm:2 * HALO + tm, :] = pooln_ref[...].astype(F32) * keep_next
    pos = si * tm + lax.broadcasted_iota(jnp.int32, (tm, 1), 0)
    pooled = []
    for gi, win in enumerate(POOL_WINDOWS):
        lanes = slice(gi * POOL_GROUP, (gi + 1) * POOL_GROUP)
        half = win // 2
        tot = pbuf[HALO - half:HALO - half + tm, lanes]
        for off in range(-half + 1, half):
            tot = tot + pbuf[HALO + off:HALO + off + tm, lanes]
        cnt = (jnp.clip(pos - half + win, 0, seq) - jnp.clip(pos - half, 0, seq)).astype(F32)
        mean_diff = (tot / cnt - pbuf[HALO:HALO + tm, lanes]).astype(BF16)
        pooled.append(jnp.dot(mean_diff, pw_ref[gi], preferred_element_type=F32))
    br_b = (jnp.concatenate(pooled, axis=1) * ps_ref[...]).astype(BF16)

    merged = None
    for n, br in enumerate((br_a, br_b, og_ref[...], od_ref[...])):
        y = jnp.dot(br, wb_ref[n], preferred_element_type=F32)
        y = gate_ref[:, n * D_MODEL:(n + 1) * D_MODEL].astype(F32) * y
        merged = y if merged is None else merged + y
    o_ref[...] = x_ref[...] + jnp.dot(merged.astype(BF16), wo_ref[...],
                                      preferred_element_type=F32)


def _merge(x2, conv, pool, og, od, gates, lw, seq, tm):
    T = x2.shape[0]
    n_s = seq // tm
    per = tm // HALO
    last = T // HALO - 1
    tok = lambda w: pl.BlockSpec((tm, w), lambda i: (i, 0))
    prev = lambda w: pl.BlockSpec((HALO, w), lambda i: (jnp.maximum(i * per - 1, 0), 0))
    nxt = lambda w: pl.BlockSpec((HALO, w), lambda i: (jnp.minimum((i + 1) * per, last), 0))
    kern = functools.partial(_merge_kernel, tm=tm, n_s=n_s, seq=seq)
    return pl.pallas_call(
        kern,
        grid=(T // tm,),
        in_specs=[tok(D_MODEL),
                  tok(3 * BRANCH_W), prev(3 * BRANCH_W), nxt(3 * BRANCH_W),
                  tok(BRANCH_W), prev(BRANCH_W), nxt(BRANCH_W),
                  tok(BRANCH_W), tok(BRANCH_W), tok(N_BRANCH * D_MODEL),
                  _resident((3, BRANCH_W)), _resident((4, POOL_GROUP, POOL_GROUP)),
                  _resident((1, BRANCH_W)), _resident((N_BRANCH, BRANCH_W, D_MODEL)),
                  _resident((D_MODEL, D_MODEL))],
        out_specs=tok(D_MODEL),
        out_shape=jax.ShapeDtypeStruct((T, D_MODEL), F32),
        scratch_shapes=[pltpu.VMEM((tm + 2 * HALO, BRANCH_W), F32),
                        pltpu.VMEM((tm + 2 * HALO, BRANCH_W), F32)],
        compiler_params=_params(1),
        name="merge",
    )(x2, conv, conv, conv, pool, pool, pool, og, od, gates,
      lw["conv_w"], lw["pool_w"], lw["pool_scale"], lw["w_branch"], lw["w_out"])


def _rope_tables(seq):
    d = np.arange(LANES) % HEAD_DIM
    t = jnp.arange(seq)

    def tables(pos, n, freq_idx, first_half, roped):
        freq = jnp.exp(-math.log(n[1]) * jnp.asarray(freq_idx, F32) * (2.0 / n[0]))
        ang = pos.astype(F32) * freq[None, :]
        c, s = jnp.cos(ang), jnp.sin(ang)
        roped, first_half = jnp.asarray(roped)[None, :], jnp.asarray(first_half)[None, :]
        return (jnp.where(roped, c, 1.0), jnp.where(roped & first_half, -s, 0.0),
                jnp.where(roped & ~first_half, s, 0.0))

    half_dim = HEAD_DIM // 2
    r = d % half_dim
    pos_axial = jnp.where(jnp.asarray(d < half_dim)[None, :], (t // GRID_W)[:, None],
                          (t % GRID_W)[:, None])
    axial = tables(pos_axial, (half_dim, AXIAL_THETA), r % (half_dim // 2), r < half_dim // 2,
                   np.ones(LANES, bool))
    partial = tables(jnp.broadcast_to(t[:, None], (seq, LANES)), (ROPE_DIM, ROPE_THETA),
                     d % (ROPE_DIM // 2), d < ROPE_DIM // 2, d < ROPE_DIM)
    return axial + partial


def _layer_weights(i, ffn1_norm, ffn1_w_in, ffn1_w_out, mix_norm, w_in, b_gate, conv_w, pool_w,
                   pool_scale, attn_q_norm, attn_k_norm, diff_q_norm, diff_k_norm, diff_lambda,
                   diff_out_norm, w_branch, w_out, ffn2_norm, ffn2_w_in, ffn2_w_out):
    two = lambda g: jnp.concatenate([g, g]).reshape(1, LANES).astype(F32)
    blk = np.arange(2 * LANES) // HEAD_DIM
    return dict(
        ffn1=(ffn1_norm[i].reshape(1, D_MODEL), ffn1_w_in[i].astype(BF16),
              ffn1_w_out[i].astype(BF16)),
        ffn2=(ffn2_norm[i].reshape(1, D_MODEL), ffn2_w_in[i].astype(BF16),
              ffn2_w_out[i].astype(BF16)),
        mix_norm=mix_norm[i].reshape(1, D_MODEL),
        w_in=w_in[i].astype(BF16),
        b_gate=b_gate[i].reshape(1, N_BRANCH * D_MODEL),
        bd=jnp.asarray(blk[:, None] == blk[None, :], BF16),
        q_norm=two(attn_q_norm[i]), k_norm=two(attn_k_norm[i]),
        dq_norm=two(diff_q_norm[i]), dk_norm=two(diff_k_norm[i]),
        diff_lambda=diff_lambda[i].astype(F32),
        diff_out_norm=diff_out_norm[i].reshape(1, DIFF_V),
        lambda_init=0.8 - 0.6 * math.exp(-0.3 * i),
        conv_w=conv_w[i], pool_w=pool_w[i].astype(BF16),
        pool_scale=pool_scale[i].reshape(1, BRANCH_W),
        w_branch=w_branch[i].astype(BF16), w_out=w_out[i].astype(BF16),
    )


def _trunk(x, layers):
    B, S, _ = x.shape
    T = B * S
    t = _tiles(S)
    tabs = _rope_tables(S)
    x2 = x.reshape(T, D_MODEL)
    for lw in layers:
        x2 = _ffn(x2, *lw["ffn1"], t["ffn_tm"])
        conv, pool, q, k, vt, dq, dk, dvt, gates = _proj(x2, lw, tabs, S, t["proj_tm"])
        seq3 = lambda a: a.reshape(B, S, a.shape[-1])
        og = _gqa(seq3(q), seq3(k), vt, t)
        od = _diff(seq3(dq), seq3(dk), dvt, lw["diff_lambda"], lw["diff_out_norm"],
                   lw["lambda_init"], t)
        x2 = _merge(x2, conv, pool, og.reshape(T, -1), od.reshape(T, -1), gates, lw, S,
                    t["merge_tm"])
        x2 = _ffn(x2, *lw["ffn2"], t["ffn_tm"])
    return x2.reshape(B, S, D_MODEL)


def kernel(x_prompt, x_sample, ffn1_norm, ffn1_w_in, ffn1_w_out, mix_norm, w_in, b_gate, conv_w, pool_w, pool_scale, attn_q_norm, attn_k_norm, diff_q_norm, diff_k_norm, diff_lambda, diff_out_norm, w_branch, w_out, ffn2_norm, ffn2_w_in, ffn2_w_out):
    weights = (ffn1_norm, ffn1_w_in, ffn1_w_out, mix_norm, w_in, b_gate, conv_w, pool_w,
               pool_scale, attn_q_norm, attn_k_norm, diff_q_norm, diff_k_norm, diff_lambda,
               diff_out_norm, w_branch, w_out, ffn2_norm, ffn2_w_in, ffn2_w_out)
    layers = [_layer_weights(i, *weights) for i in range(DEPTH)]
    return (_trunk(x_prompt, layers), _trunk(x_sample, layers))
```

```python
import functools
import math

import jax
import jax.numpy as jnp
import numpy as np
from jax import lax
from jax.experimental import pallas as pl
from jax.experimental.pallas import tpu as pltpu

F32 = jnp.float32
BF16 = jnp.bfloat16

D_MODEL = 1024
DEPTH = 2
GRID_W = 64
HEAD_DIM = 64
D_FF = 2816
N_BRANCH = 4
BRANCH_W = 512
POOL_WINDOWS = (2, 4, 8, 16)
POOL_GROUP = BRANCH_W // 4
GQA_HEADS = 8
GQA_KV = 2
DIFF_HEADS = 4
DIFF_V = 2 * HEAD_DIM
AXIAL_THETA = 10000.0
ROPE_THETA = 500000.0
ROPE_DIM = HEAD_DIM // 4
EPS = 1e-6

_OFF_CONV = 0
_OFF_POOL = 3 * BRANCH_W
_OFF_CQ = _OFF_POOL + BRANCH_W
_OFF_CK = _OFF_CQ + GQA_HEADS * HEAD_DIM
_OFF_CV = _OFF_CK + GQA_KV * HEAD_DIM
_OFF_DQ = _OFF_CV + GQA_KV * HEAD_DIM
_OFF_DK = _OFF_DQ + DIFF_HEADS * 2 * HEAD_DIM
_OFF_DV = _OFF_DK + DIFF_HEADS * 2 * HEAD_DIM
_OFF_GATE = _OFF_DV + DIFF_HEADS * DIFF_V
IN_W = _OFF_GATE + N_BRANCH * D_MODEL

LANES = 128
BF16_SUBLANES = 16
VMEM_LIMIT_BYTES = 56 * 1024 * 1024

HALO = BF16_SUBLANES
ONES_ROWS = BF16_SUBLANES
NEG_BIG = -1e30
MXU_DEPTH = 256
SCORE_TILE_ELEMS = 1024 * 2048


def _tiles(seq):
    tkv = min(1024, seq // 4)
    cols = SCORE_TILE_ELEMS // tkv
    return dict(
        ffn_tm=512,
        proj_tm=256,
        merge_tm=512,
        gqa_tq=min(cols // GQA_HEADS, seq),
        diff_tq=min(cols // 2, seq),
        tkv=tkv,
    )


def _resident(shape):
    zeros = (0,) * len(shape)
    return pl.BlockSpec(shape, lambda *_: zeros, pipeline_mode=pl.Buffered(1))


def _params(n_axes):
    return pltpu.CompilerParams(dimension_semantics=("arbitrary",) * n_axes,
                                vmem_limit_bytes=VMEM_LIMIT_BYTES)


def _rms(xf, gain):
    return xf * lax.rsqrt(jnp.mean(xf * xf, axis=-1, keepdims=True) + EPS) * gain


def _ffn_kernel(x_ref, g_ref, win_ref, wout_ref, o_ref):
    x = x_ref[...]
    u = _rms(x, g_ref[...]).astype(BF16)
    h = jnp.dot(u, win_ref[...], preferred_element_type=F32)
    a = h[:, :D_FF]
    b = h[:, D_FF:]
    act = (a * jax.nn.sigmoid(a) * b).astype(BF16)
    y = jnp.dot(act, wout_ref[...], preferred_element_type=F32)
    o_ref[...] = x + 0.5 * y


def _ffn(x2, gain, w_in, w_out, tm):
    T = x2.shape[0]
    return pl.pallas_call(
        _ffn_kernel,
        grid=(T // tm,),
        in_specs=[pl.BlockSpec((tm, D_MODEL), lambda i: (i, 0)),
                  _resident((1, D_MODEL)),
                  _resident((D_MODEL, 2 * D_FF)),
                  _resident((D_FF, D_MODEL))],
        out_specs=pl.BlockSpec((tm, D_MODEL), lambda i: (i, 0)),
        out_shape=jax.ShapeDtypeStruct((T, D_MODEL), F32),
        compiler_params=_params(1),
        name="ffn",
    )(x2, gain, w_in, w_out)


def _head_sum_squares(z, bd):
    return jnp.dot((z * z).astype(BF16), bd, preferred_element_type=F32)


def _head_norm_rope(z, ss, gain, cos, sin_a, sin_b, shift, scale):
    zn = z * lax.rsqrt(ss * (1.0 / HEAD_DIM) + EPS) * gain
    out = zn * cos + pltpu.roll(zn, LANES - shift, 1) * sin_a + pltpu.roll(zn, shift, 1) * sin_b
    if scale != 1.0:
        out = out * scale
    return out.astype(BF16)


def _proj_kernel(x_ref, g_ref, w_ref, bgate_ref, bd_ref, qn_ref, kn_ref, dqn_ref, dkn_ref,
                 acos_ref, asa_ref, asb_ref, pcos_ref, psa_ref, psb_ref,
                 conv_o, pool_o, q_o, k_o, vt_o, dq_o, dk_o, dvt_o, gate_o):
    u = _rms(x_ref[...], g_ref[...]).astype(BF16)

    def proj(lo, width):
        return jnp.dot(u, w_ref[:, lo:lo + width], preferred_element_type=F32)

    conv_o[...] = proj(_OFF_CONV, 3 * BRANCH_W).astype(BF16)
    pool_o[...] = proj(_OFF_POOL, BRANCH_W).astype(BF16)

    bd = bd_ref[...]
    axial = (acos_ref[...], asa_ref[...], asb_ref[...], HEAD_DIM // 4)
    partial = (pcos_ref[...], psa_ref[...], psb_ref[...], ROPE_DIM // 2)
    q_scale = HEAD_DIM ** -0.5 * math.log2(math.e)

    def heads(z, width, out_ref, gain_ref, tables, scale):
        gain = gain_ref[...]
        for j in range(0, width, 2 * LANES):
            ss = _head_sum_squares(z[:, j:j + 2 * LANES], bd)
            for lo in range(j, min(j + 2 * LANES, width), LANES):
                sl = slice(lo, lo + LANES)
                out_ref[:, sl] = _head_norm_rope(z[:, sl], ss[:, lo - j:lo - j + LANES], gain,
                                                 *tables, scale)

    def values_t(v, n_heads, ch, out_ref):
        vt = v.T.astype(BF16)
        ones = jnp.ones((ONES_ROWS, vt.shape[1]), BF16)
        rows = ch + ONES_ROWS
        for h in range(n_heads):
            out_ref[h * rows:h * rows + ch, :] = vt[h * ch:(h + 1) * ch, :]
            out_ref[h * rows + ch:(h + 1) * rows, :] = ones

    n_q, n_k = GQA_HEADS * HEAD_DIM, GQA_KV * HEAD_DIM
    heads(proj(_OFF_CQ, n_q), n_q, q_o, qn_ref, axial, q_scale)
    kv = proj(_OFF_CK, 2 * n_k)
    heads(kv, n_k, k_o, kn_ref, axial, 1.0)
    values_t(kv[:, n_k:], GQA_KV, HEAD_DIM, vt_o)
    n_d = DIFF_HEADS * 2 * HEAD_DIM
    heads(proj(_OFF_DQ, n_d), n_d, dq_o, dqn_ref, partial, q_scale)
    heads(proj(_OFF_DK, n_d), n_d, dk_o, dkn_ref, partial, 1.0)
    values_t(proj(_OFF_DV, DIFF_HEADS * DIFF_V), DIFF_HEADS, DIFF_V, dvt_o)

    for n in range(N_BRANCH):
        sl = slice(n * D_MODEL, (n + 1) * D_MODEL)
        g = proj(_OFF_GATE + n * D_MODEL, D_MODEL) + bgate_ref[:, sl]
        gate_o[:, sl] = jax.nn.sigmoid(g).astype(BF16)


def _proj(x2, lw, tabs, seq, tm):
    T = x2.shape[0]
    n_s = seq // tm
    tok = lambda w: pl.BlockSpec((tm, w), lambda i: (i, 0))
    tab = pl.BlockSpec((tm, LANES), lambda i: (i % n_s, 0))
    tok_t = lambda w: pl.BlockSpec((w, tm), lambda i: (0, i))
    outs = ((3 * BRANCH_W, False), (BRANCH_W, False), (GQA_HEADS * HEAD_DIM, False),
            (GQA_KV * HEAD_DIM, False), (GQA_KV * (HEAD_DIM + ONES_ROWS), True),
            (DIFF_HEADS * 2 * HEAD_DIM, False), (DIFF_HEADS * 2 * HEAD_DIM, False),
            (DIFF_HEADS * (DIFF_V + ONES_ROWS), True), (N_BRANCH * D_MODEL, False))
    return pl.pallas_call(
        _proj_kernel,
        grid=(T // tm,),
        in_specs=[tok(D_MODEL), _resident((1, D_MODEL)), _resident((D_MODEL, IN_W)),
                  _resident((1, N_BRANCH * D_MODEL)), _resident((2 * LANES, 2 * LANES)),
                  _resident((1, LANES)), _resident((1, LANES)), _resident((1, LANES)),
                  _resident((1, LANES))] + [tab] * 6,
        out_specs=[tok_t(w) if tr else tok(w) for w, tr in outs],
        out_shape=[jax.ShapeDtypeStruct((w, T) if tr else (T, w), BF16) for w, tr in outs],
        compiler_params=_params(1),
        name="proj",
    )(x2, lw["mix_norm"], lw["w_in"], lw["b_gate"], lw["bd"], lw["q_norm"], lw["k_norm"],
      lw["dq_norm"], lw["dk_norm"], *tabs)


def _flash_cols(qst_ref, k_ref, vt_ref, m_ref, acc_ref, s_ref, smax_ref, alpha_ref, p_ref,
                *, tkv, n_kv, pv_groups):
    m_ref[...] = jnp.full(m_ref.shape, NEG_BIG, F32)
    acc_ref[...] = jnp.zeros(acc_ref.shape, F32)

    R = m_ref.shape[1]
    ck = min(MXU_DEPTH, tkv)
    n_chunks = tkv // ck

    def kv_start(u, c):
        base = u * tkv if isinstance(u, int) else pl.multiple_of(u * tkv, tkv)
        return base + c * ck

    def slot(scores_of=None, softmax=False, pv_of=None):
        if pv_of is not None:
            alpha = alpha_ref[...]
        if softmax:
            m_prev = m_ref[...]
            m_new = jnp.maximum(m_prev, smax_ref[...])
            alpha_ref[...] = jnp.exp2(m_prev - m_new)
            m_ref[...] = m_new
        col_max = None
        for c in range(n_chunks):
            rows = slice(c * ck, (c + 1) * ck)
            if pv_of is not None:
                kv = pl.ds(kv_start(pv_of, c), ck)
                for row0, cols in pv_groups:
                    vt = vt_ref[row0:row0 + acc_ref.shape[0], kv]
                    part = jnp.dot(vt, p_ref[rows, cols], preferred_element_type=F32)
                    if c == 0:
                        acc_ref[:, cols] = alpha[:, cols] * acc_ref[:, cols] + part
                    else:
                        acc_ref[:, cols] += part
            if softmax:
                p_ref[rows, :] = jnp.exp2(s_ref[rows, :R] - m_new).astype(BF16)
            if scores_of is not None:
                k = k_ref[0, pl.ds(kv_start(scores_of, c), ck), :]
                s = jnp.dot(k, qst_ref[...], preferred_element_type=F32)
                s_ref[rows, :R] = s
                part = jnp.max(s, axis=0, keepdims=True)
                col_max = part if col_max is None else jnp.maximum(col_max, part)
        if scores_of is not None:
            smax_ref[...] = col_max

    slot(scores_of=0)
    slot(scores_of=1, softmax=True)

    def steady(u, carry):
        slot(scores_of=u, softmax=True, pv_of=u - 2)
        return carry

    lax.fori_loop(2, n_kv, steady, 0, unroll=2 if n_kv - 2 >= 4 else 1)
    slot(softmax=True, pv_of=n_kv - 2)
    slot(pv_of=n_kv - 1)


def _stack_queries(x, qst_ref, cols, k_halves):
    xt = x.astype(F32).T
    zero = jnp.zeros((HEAD_DIM, xt.shape[1]), F32)
    for part, c, k_half in zip((xt[:HEAD_DIM], xt[HEAD_DIM:]), cols, k_halves):
        block = [part, zero] if k_half == 0 else [zero, part]
        qst_ref[:, c] = jnp.concatenate(block, axis=0).astype(BF16)


def _attn_scratch(R, tkv, acc_rows):
    return [pltpu.VMEM((LANES, R), BF16), pltpu.VMEM((1, R), F32),
            pltpu.VMEM((acc_rows, R + LANES), F32), pltpu.VMEM((tkv, R + LANES), F32),
            pltpu.VMEM((1, R), F32), pltpu.VMEM((1, R), F32),
            pltpu.VMEM((tkv, R), BF16)]


def _gqa_kernel(q_ref, k_ref, vt_ref, o_ref, qst_ref, m_ref, acc_ref, *pipe, tq, tkv, n_kv):
    per_kv = GQA_HEADS // GQA_KV
    col = lambda h: slice(h * tq, (h + 1) * tq)
    for j in range(GQA_HEADS // 2):
        heads = (2 * j, 2 * j + 1)
        _stack_queries(q_ref[0, :, j * LANES:(j + 1) * LANES], qst_ref,
                       [col(h) for h in heads], [h // per_kv for h in heads])
    half = per_kv * tq
    _flash_cols(qst_ref, k_ref, vt_ref, m_ref, acc_ref, *pipe, tkv=tkv, n_kv=n_kv,
                pv_groups=((0, slice(0, half)), (HEAD_DIM + ONES_ROWS, slice(half, 2 * half))))
    out, den = slice(0, HEAD_DIM), slice(HEAD_DIM, HEAD_DIM + 1)
    for j in range(GQA_HEADS // 2):
        c0, c1 = col(2 * j), col(2 * j + 1)
        top = acc_ref[out, c0] * (1.0 / acc_ref[den, c0])
        bot = acc_ref[out, c1] * (1.0 / acc_ref[den, c1])
        o_ref[0, :, j * LANES:(j + 1) * LANES] = (
            jnp.concatenate([top, bot], axis=0).T.astype(BF16))


def _gqa(q, k, vt, t):
    B, S, W = q.shape
    tq, tkv = t["gqa_tq"], t["tkv"]
    kern = functools.partial(_gqa_kernel, tq=tq, tkv=tkv, n_kv=S // tkv)
    return pl.pallas_call(
        kern,
        grid=(B, S // tq),
        in_specs=[pl.BlockSpec((1, tq, W), lambda b, i: (b, i, 0)),
                  pl.BlockSpec((1, S, LANES), lambda b, i: (b, 0, 0)),
                  pl.BlockSpec((GQA_KV * (HEAD_DIM + ONES_ROWS), S), lambda b, i: (0, b))],
        out_specs=pl.BlockSpec((1, tq, W), lambda b, i: (b, i, 0)),
        out_shape=jax.ShapeDtypeStruct((B, S, W), BF16),
        scratch_shapes=_attn_scratch(GQA_HEADS * tq, tkv, HEAD_DIM + ONES_ROWS),
        compiler_params=_params(2),
        name="gqa",
    )(q, k, vt)


def _diff_kernel(lam_ref, on_ref, q_ref, k_ref, vt_ref, o_ref, qst_ref, m_ref, acc_ref,
                 *pipe, tq, tkv, n_kv, lambda_init):
    c0, c1 = slice(0, tq), slice(tq, 2 * tq)
    _stack_queries(q_ref[0], qst_ref, (c0, c1), (0, 1))
    _flash_cols(qst_ref, k_ref, vt_ref, m_ref, acc_ref, *pipe, tkv=tkv, n_kv=n_kv,
                pv_groups=((0, slice(0, 2 * tq)),))
    lv = lam_ref[...]
    lam = (jnp.exp(jnp.sum(lv[0:1] * lv[1:2], axis=1, keepdims=True))
           - jnp.exp(jnp.sum(lv[2:3] * lv[3:4], axis=1, keepdims=True)) + lambda_init)
    out, den = slice(0, DIFF_V), slice(DIFF_V, DIFF_V + 1)
    ot = (acc_ref[out, c0] * (1.0 / acc_ref[den, c0])
          - acc_ref[out, c1] * (lam / acc_ref[den, c1]))
    otn = ot * lax.rsqrt(jnp.mean(ot * ot, axis=0, keepdims=True) + EPS)
    o_ref[0] = (otn.T * on_ref[...] * (1.0 - lambda_init)).astype(BF16)


def _diff(q, k, vt, lam, out_norm, lambda_init, t):
    B, S, W = q.shape
    tq, tkv = t["diff_tq"], t["tkv"]
    kern = functools.partial(_diff_kernel, tq=tq, tkv=tkv, n_kv=S // tkv,
                             lambda_init=lambda_init)
    return pl.pallas_call(
        kern,
        grid=(B, DIFF_HEADS, S // tq),
        in_specs=[pl.BlockSpec((4, HEAD_DIM), lambda b, h, i: (0, 0)),
                  pl.BlockSpec((1, DIFF_V), lambda b, h, i: (0, 0)),
                  pl.BlockSpec((1, tq, LANES), lambda b, h, i: (b, i, h)),
                  pl.BlockSpec((1, S, LANES), lambda b, h, i: (b, 0, h)),
                  pl.BlockSpec((DIFF_V + ONES_ROWS, S), lambda b, h, i: (h, b))],
        out_specs=pl.BlockSpec((1, tq, LANES), lambda b, h, i: (b, i, h)),
        out_shape=jax.ShapeDtypeStruct((B, S, W), BF16),
        scratch_shapes=_attn_scratch(2 * tq, tkv, DIFF_V + ONES_ROWS),
        compiler_params=_params(3),
        name="diff",
    )(lam, out_norm, q, k, vt)


def _merge_kernel(x_ref, conv_ref, convp_ref, convn_ref, pool_ref, poolp_ref, pooln_ref,
                  og_ref, od_ref, gate_ref, cw_ref, pw_ref, ps_ref, wb_ref, wo_ref, o_ref,
                  cbuf, pbuf, *, tm, n_s, seq):
    si = pl.program_id(0) % n_s
    keep_prev = (si > 0).astype(F32)
    keep_next = (si < n_s - 1).astype(F32)
    W = BRANCH_W

    def gated(ref):
        blk = ref[...].astype(F32)
        return blk[:, 2 * W:3 * W] * blk[:, 0:W]
    cbuf[0:HALO, :] = gated(convp_ref) * keep_prev
    cbuf[HALO:HALO + tm, :] = gated(conv_ref)
    cbuf[HALO + tm:2 * HALO + tm, :] = gated(convn_ref) * keep_next
    cw = cw_ref[...]
    conv = (cw[0:1] * cbuf[HALO - 1:HALO - 1 + tm, :] + cw[1:2] * cbuf[HALO:HALO + tm, :]
            + cw[2:3] * cbuf[HALO + 1:HALO + 1 + tm, :])
    br_a = (conv_ref[:, W:2 * W].astype(F32) * conv).astype(BF16)

    pbuf[0:HALO, :] = poolp_ref[...].astype(F32) * keep_prev
    pbuf[HALO:HALO + tm, :] = pool_ref[...].astype(F32)
    pbuf[HALO + tm:2 * HALO + tm, :] = pooln_ref[...].astype(F32) * keep_next
    pos = si * tm + lax.broadcasted_iota(jnp.int32, (tm, 1), 0)
    pooled = []
    for gi, win in enumerate(POOL_WINDOWS):
        lanes = slice(gi * POOL_GROUP, (gi + 1) * POOL_GROUP)
        half = win // 2
        tot = pbuf[HALO - half:HALO - half + tm, lanes]
        for off in range(-half + 1, half):
            tot = tot + pbuf[HALO + off:HALO + off + tm, lanes]
        cnt = (jnp.clip(pos - half + win, 0, seq) - jnp.clip(pos - half, 0, seq)).astype(F32)
        mean_diff = (tot / cnt - pbuf[HALO:HALO + tm, lanes]).astype(BF16)
        pooled.append(jnp.dot(mean_diff, pw_ref[gi], preferred_element_type=F32))
    br_b = (jnp.concatenate(pooled, axis=1) * ps_ref[...]).astype(BF16)

    merged = None
    for n, br in enumerate((br_a, br_b, og_ref[...], od_ref[...])):
        y = jnp.dot(br, wb_ref[n], preferred_element_type=F32)
        y = gate_ref[:, n * D_MODEL:(n + 1) * D_MODEL].astype(F32) * y
        merged = y if merged is None else merged + y
    o_ref[...] = x_ref[...] + jnp.dot(merged.astype(BF16), wo_ref[...],
                                      preferred_element_type=F32)


def _merge(x2, conv, pool, og, od, gates, lw, seq, tm):
    T = x2.shape[0]
    n_s = seq // tm
    per = tm // HALO
    last = T // HALO - 1
    tok = lambda w: pl.BlockSpec((tm, w), lambda i: (i, 0))
    prev = lambda w: pl.BlockSpec((HALO, w), lambda i: (jnp.maximum(i * per - 1, 0), 0))
    nxt = lambda w: pl.BlockSpec((HALO, w), lambda i: (jnp.minimum((i + 1) * per, last), 0))
    kern = functools.partial(_merge_kernel, tm=tm, n_s=n_s, seq=seq)
    return pl.pallas_call(
        kern,
        grid=(T // tm,),
        in_specs=[tok(D_MODEL),
                  tok(3 * BRANCH_W), prev(3 * BRANCH_W), nxt(3 * BRANCH_W),
                  tok(BRANCH_W), prev(BRANCH_W), nxt(BRANCH_W),
                  tok(BRANCH_W), tok(BRANCH_W), tok(N_BRANCH * D_MODEL),
                  _resident((3, BRANCH_W)), _resident((4, POOL_GROUP, POOL_GROUP)),
                  _resident((1, BRANCH_W)), _resident((N_BRANCH, BRANCH_W, D_MODEL)),
                  _resident((D_MODEL, D_MODEL))],
        out_specs=tok(D_MODEL),
        out_shape=jax.ShapeDtypeStruct((T, D_MODEL), F32),
        scratch_shapes=[pltpu.VMEM((tm + 2 * HALO, BRANCH_W), F32),
                        pltpu.VMEM((tm + 2 * HALO, BRANCH_W), F32)],
        compiler_params=_params(1),
        name="merge",
    )(x2, conv, conv, conv, pool, pool, pool, og, od, gates,
      lw["conv_w"], lw["pool_w"], lw["pool_scale"], lw["w_branch"], lw["w_out"])


def _rope_tables(seq):
    d = np.arange(LANES) % HEAD_DIM
    t = jnp.arange(seq)

    def tables(pos, n, freq_idx, first_half, roped):
        freq = jnp.exp(-math.log(n[1]) * jnp.asarray(freq_idx, F32) * (2.0 / n[0]))
        ang = pos.astype(F32) * freq[None, :]
        c, s = jnp.cos(ang), jnp.sin(ang)
        roped, first_half = jnp.asarray(roped)[None, :], jnp.asarray(first_half)[None, :]
        return (jnp.where(roped, c, 1.0), jnp.where(roped & first_half, -s, 0.0),
                jnp.where(roped & ~first_half, s, 0.0))

    half_dim = HEAD_DIM // 2
    r = d % half_dim
    pos_axial = jnp.where(jnp.asarray(d < half_dim)[None, :], (t // GRID_W)[:, None],
                          (t % GRID_W)[:, None])
    axial = tables(pos_axial, (half_dim, AXIAL_THETA), r % (half_dim // 2), r < half_dim // 2,
                   np.ones(LANES, bool))
    partial = tables(jnp.broadcast_to(t[:, None], (seq, LANES)), (ROPE_DIM, ROPE_THETA),
                     d % (ROPE_DIM // 2), d < ROPE_DIM // 2, d < ROPE_DIM)
    return axial + partial


def _layer_weights(i, ffn1_norm, ffn1_w_in, ffn1_w_out, mix_norm, w_in, b_gate, conv_w, pool_w,
                   pool_scale, attn_q_norm, attn_k_norm, diff_q_norm, diff_k_norm, diff_lambda,
                   diff_out_norm, w_branch, w_out, ffn2_norm, ffn2_w_in, ffn2_w_out):
    two = lambda g: jnp.concatenate([g, g]).reshape(1, LANES).astype(F32)
    blk = np.arange(2 * LANES) // HEAD_DIM
    return dict(
        ffn1=(ffn1_norm[i].reshape(1, D_MODEL), ffn1_w_in[i].astype(BF16),
              ffn1_w_out[i].astype(BF16)),
        ffn2=(ffn2_norm[i].reshape(1, D_MODEL), ffn2_w_in[i].astype(BF16),
              ffn2_w_out[i].astype(BF16)),
        mix_norm=mix_norm[i].reshape(1, D_MODEL),
        w_in=w_in[i].astype(BF16),
        b_gate=b_gate[i].reshape(1, N_BRANCH * D_MODEL),
        bd=jnp.asarray(blk[:, None] == blk[None, :], BF16),
        q_norm=two(attn_q_norm[i]), k_norm=two(attn_k_norm[i]),
        dq_norm=two(diff_q_norm[i]), dk_norm=two(diff_k_norm[i]),
        diff_lambda=diff_lambda[i].astype(F32),
        diff_out_norm=diff_out_norm[i].reshape(1, DIFF_V),
        lambda_init=0.8 - 0.6 * math.exp(-0.3 * i),
        conv_w=conv_w[i], pool_w=pool_w[i].astype(BF16),
        pool_scale=pool_scale[i].reshape(1, BRANCH_W),
        w_branch=w_branch[i].astype(BF16), w_out=w_out[i].astype(BF16),
    )


def _trunk(x, layers):
    B, S, _ = x.shape
    T = B * S
    t = _tiles(S)
    tabs = _rope_tables(S)
    x2 = x.reshape(T, D_MODEL)
    for lw in layers:
        x2 = _ffn(x2, *lw["ffn1"], t["ffn_tm"])
        conv, pool, q, k, vt, dq, dk, dvt, gates = _proj(x2, lw, tabs, S, t["proj_tm"])
        seq3 = lambda a: a.reshape(B, S, a.shape[-1])
        og = _gqa(seq3(q), seq3(k), vt, t)
        od = _diff(seq3(dq), seq3(dk), dvt, lw["diff_lambda"], lw["diff_out_norm"],
                   lw["lambda_init"], t)
        x2 = _merge(x2, conv, pool, og.reshape(T, -1), od.reshape(T, -1), gates, lw, S,
                    t["merge_tm"])
        x2 = _ffn(x2, *lw["ffn2"], t["ffn_tm"])
    return x2.reshape(B, S, D_MODEL)


def kernel(x_prompt, x_sample, ffn1_norm, ffn1_w_in, ffn1_w_out, mix_norm, w_in, b_gate, conv_w, pool_w, pool_scale, attn_q_norm, attn_k_norm, diff_q_norm, diff_k_norm, diff_lambda, diff_out_norm, w_branch, w_out, ffn2_norm, ffn2_w_in, ffn2_w_out):
    weights = (ffn1_norm, ffn1_w_in, ffn1_w_out, mix_norm, w_in, b_gate, conv_w, pool_w,
               pool_scale, attn_q_norm, attn_k_norm, diff_q_norm, diff_k_norm, diff_lambda,
               diff_out_norm, w_branch, w_out, ffn2_norm, ffn2_w_in, ffn2_w_out)
    layers = [_layer_weights(i, *weights) for i in range(DEPTH)]
    return (_trunk(x_prompt, layers), _trunk(x_sample, layers))
```

```python
import functools
import math

import jax
import jax.numpy as jnp
import numpy as np
from jax import lax
from jax.experimental import pallas as pl
from jax.experimental.pallas import tpu as pltpu

F32 = jnp.float32
BF16 = jnp.bfloat16

D_MODEL = 1024
DEPTH = 2
GRID_W = 64
HEAD_DIM = 64
D_FF = 2816
N_BRANCH = 4
BRANCH_W = 512
POOL_WINDOWS = (2, 4, 8, 16)
POOL_GROUP = BRANCH_W // 4
GQA_HEADS = 8
GQA_KV = 2
DIFF_HEADS = 4
DIFF_V = 2 * HEAD_DIM
AXIAL_THETA = 10000.0
ROPE_THETA = 500000.0
ROPE_DIM = HEAD_DIM // 4
EPS = 1e-6

_OFF_CONV = 0
_OFF_POOL = 3 * BRANCH_W
_OFF_CQ = _OFF_POOL + BRANCH_W
_OFF_CK = _OFF_CQ + GQA_HEADS * HEAD_DIM
_OFF_CV = _OFF_CK + GQA_KV * HEAD_DIM
_OFF_DQ = _OFF_CV + GQA_KV * HEAD_DIM
_OFF_DK = _OFF_DQ + DIFF_HEADS * 2 * HEAD_DIM
_OFF_DV = _OFF_DK + DIFF_HEADS * 2 * HEAD_DIM
_OFF_GATE = _OFF_DV + DIFF_HEADS * DIFF_V
IN_W = _OFF_GATE + N_BRANCH * D_MODEL

LANES = 128
BF16_SUBLANES = 16
VMEM_LIMIT_BYTES = 56 * 1024 * 1024

HALO = BF16_SUBLANES
ONES_ROWS = BF16_SUBLANES
NEG_BIG = -1e30
POOL_BLOCK = 128
MXU_DEPTH = 256
SCORE_TILE_ELEMS = 1024 * 2048


def _tiles(seq):
    tkv = min(1024, seq // 4)
    cols = SCORE_TILE_ELEMS // tkv
    return dict(
        ffn_tm=512,
        proj_tm=256,
        merge_tm=512,
        gqa_tq=min(cols // GQA_HEADS, seq),
        diff_tq=min(cols // 2, seq),
        tkv=tkv,
    )


def _resident(shape):
    zeros = (0,) * len(shape)
    return pl.BlockSpec(shape, lambda *_: zeros, pipeline_mode=pl.Buffered(1))


def _params(n_axes):
    return pltpu.CompilerParams(dimension_semantics=("arbitrary",) * n_axes,
                                vmem_limit_bytes=VMEM_LIMIT_BYTES)


def _rms(xf, gain):
    return xf * lax.rsqrt(jnp.mean(xf * xf, axis=-1, keepdims=True) + EPS) * gain


def _ffn_kernel(x_ref, g_ref, win_ref, wout_ref, o_ref):
    x = x_ref[...]
    u = _rms(x, g_ref[...]).astype(BF16)
    h = jnp.dot(u, win_ref[...], preferred_element_type=F32)
    a = h[:, :D_FF]
    b = h[:, D_FF:]
    act = (a * jax.nn.sigmoid(a) * b).astype(BF16)
    y = jnp.dot(act, wout_ref[...], preferred_element_type=F32)
    o_ref[...] = x + 0.5 * y


def _ffn(x2, gain, w_in, w_out, tm):
    T = x2.shape[0]
    return pl.pallas_call(
        _ffn_kernel,
        grid=(T // tm,),
        in_specs=[pl.BlockSpec((tm, D_MODEL), lambda i: (i, 0)),
                  _resident((1, D_MODEL)),
                  _resident((D_MODEL, 2 * D_FF)),
                  _resident((D_FF, D_MODEL))],
        out_specs=pl.BlockSpec((tm, D_MODEL), lambda i: (i, 0)),
        out_shape=jax.ShapeDtypeStruct((T, D_MODEL), F32),
        compiler_params=_params(1),
        name="ffn",
    )(x2, gain, w_in, w_out)


def _head_sum_squares(z, bd):
    return jnp.dot((z * z).astype(BF16), bd, preferred_element_type=F32)


def _head_norm_rope(z, ss, gain, cos, sin_a, sin_b, shift, scale):
    zn = z * lax.rsqrt(ss * (1.0 / HEAD_DIM) + EPS) * gain
    out = zn * cos + pltpu.roll(zn, LANES - shift, 1) * sin_a + pltpu.roll(zn, shift, 1) * sin_b
    if scale != 1.0:
        out = out * scale
    return out.astype(BF16)


def _proj_kernel(x_ref, g_ref, w_ref, bgate_ref, bd_ref, qn_ref, kn_ref, dqn_ref, dkn_ref,
                 acos_ref, asa_ref, asb_ref, pcos_ref, psa_ref, psb_ref,
                 conv_o, pool_o, q_o, k_o, vt_o, dq_o, dk_o, dvt_o, gate_o):
    u = _rms(x_ref[...], g_ref[...]).astype(BF16)

    def proj(lo, width):
        return jnp.dot(u, w_ref[:, lo:lo + width], preferred_element_type=F32)

    conv_o[...] = proj(_OFF_CONV, 3 * BRANCH_W).astype(BF16)
    pool_o[...] = proj(_OFF_POOL, BRANCH_W).astype(BF16)

    bd = bd_ref[...]
    axial = (acos_ref[...], asa_ref[...], asb_ref[...], HEAD_DIM // 4)
    partial = (pcos_ref[...], psa_ref[...], psb_ref[...], ROPE_DIM // 2)
    q_scale = HEAD_DIM ** -0.5 * math.log2(math.e)

    def heads(z, width, out_ref, gain_ref, tables, scale):
        gain = gain_ref[...]
        for j in range(0, width, 2 * LANES):
            ss = _head_sum_squares(z[:, j:j + 2 * LANES], bd)
            for lo in range(j, min(j + 2 * LANES, width), LANES):
                sl = slice(lo, lo + LANES)
                out_ref[:, sl] = _head_norm_rope(z[:, sl], ss[:, lo - j:lo - j + LANES], gain,
                                                 *tables, scale)

    def values_t(v, n_heads, ch, out_ref):
        vt = v.T.astype(BF16)
        ones = jnp.ones((ONES_ROWS, vt.shape[1]), BF16)
        rows = ch + ONES_ROWS
        for h in range(n_heads):
            out_ref[h * rows:h * rows + ch, :] = vt[h * ch:(h + 1) * ch, :]
            out_ref[h * rows + ch:(h + 1) * rows, :] = ones

    n_q, n_k = GQA_HEADS * HEAD_DIM, GQA_KV * HEAD_DIM
    heads(proj(_OFF_CQ, n_q), n_q, q_o, qn_ref, axial, q_scale)
    kv = proj(_OFF_CK, 2 * n_k)
    heads(kv, n_k, k_o, kn_ref, axial, 1.0)
    values_t(kv[:, n_k:], GQA_KV, HEAD_DIM, vt_o)
    n_d = DIFF_HEADS * 2 * HEAD_DIM
    heads(proj(_OFF_DQ, n_d), n_d, dq_o, dqn_ref, partial, q_scale)
    heads(proj(_OFF_DK, n_d), n_d, dk_o, dkn_ref, partial, 1.0)
    values_t(proj(_OFF_DV, DIFF_HEADS * DIFF_V), DIFF_HEADS, DIFF_V, dvt_o)

    for n in range(N_BRANCH):
        sl = slice(n * D_MODEL, (n + 1) * D_MODEL)
        g = proj(_OFF_GATE + n * D_MODEL, D_MODEL) + bgate_ref[:, sl]
        gate_o[:, sl] = jax.nn.sigmoid(g).astype(BF16)


def _proj(x2, lw, tabs, seq, tm):
    T = x2.shape[0]
    n_s = seq // tm
    tok = lambda w: pl.BlockSpec((tm, w), lambda i: (i, 0))
    tab = pl.BlockSpec((tm, LANES), lambda i: (i % n_s, 0))
    tok_t = lambda w: pl.BlockSpec((w, tm), lambda i: (0, i))
    outs = ((3 * BRANCH_W, False), (BRANCH_W, False), (GQA_HEADS * HEAD_DIM, False),
            (GQA_KV * HEAD_DIM, False), (GQA_KV * (HEAD_DIM + ONES_ROWS), True),
            (DIFF_HEADS * 2 * HEAD_DIM, False), (DIFF_HEADS * 2 * HEAD_DIM, False),
            (DIFF_HEADS * (DIFF_V + ONES_ROWS), True), (N_BRANCH * D_MODEL, False))
    return pl.pallas_call(
        _proj_kernel,
        grid=(T // tm,),
        in_specs=[tok(D_MODEL), _resident((1, D_MODEL)), _resident((D_MODEL, IN_W)),
                  _resident((1, N_BRANCH * D_MODEL)), _resident((2 * LANES, 2 * LANES)),
                  _resident((1, LANES)), _resident((1, LANES)), _resident((1, LANES)),
                  _resident((1, LANES))] + [tab] * 6,
        out_specs=[tok_t(w) if tr else tok(w) for w, tr in outs],
        out_shape=[jax.ShapeDtypeStruct((w, T) if tr else (T, w), BF16) for w, tr in outs],
        compiler_params=_params(1),
        name="proj",
    )(x2, lw["mix_norm"], lw["w_in"], lw["b_gate"], lw["bd"], lw["q_norm"], lw["k_norm"],
      lw["dq_norm"], lw["dk_norm"], *tabs)


def _flash_cols(qst_ref, k_ref, vt_ref, m_ref, acc_ref, s_ref, smax_ref, alpha_ref, p_ref,
                *, tkv, n_kv, pv_groups):
    m_ref[...] = jnp.full(m_ref.shape, NEG_BIG, F32)
    acc_ref[...] = jnp.zeros(acc_ref.shape, F32)

    R = m_ref.shape[1]
    ck = min(MXU_DEPTH, tkv)
    n_chunks = tkv // ck

    def kv_start(u, c):
        base = u * tkv if isinstance(u, int) else pl.multiple_of(u * tkv, tkv)
        return base + c * ck

    def slot(scores_of=None, softmax=False, pv_of=None):
        if pv_of is not None:
            alpha = alpha_ref[...]
        if softmax:
            m_prev = m_ref[...]
            m_new = jnp.maximum(m_prev, smax_ref[...])
            alpha_ref[...] = jnp.exp2(m_prev - m_new)
            m_ref[...] = m_new
        col_max = None
        for c in range(n_chunks):
            rows = slice(c * ck, (c + 1) * ck)
            if pv_of is not None:
                kv = pl.ds(kv_start(pv_of, c), ck)
                for row0, cols in pv_groups:
                    vt = vt_ref[row0:row0 + acc_ref.shape[0], kv]
                    part = jnp.dot(vt, p_ref[rows, cols], preferred_element_type=F32)
                    if c == 0:
                        acc_ref[:, cols] = alpha[:, cols] * acc_ref[:, cols] + part
                    else:
                        acc_ref[:, cols] += part
            if softmax:
                p_ref[rows, :] = jnp.exp2(s_ref[rows, :R] - m_new).astype(BF16)
            if scores_of is not None:
                k = k_ref[0, pl.ds(kv_start(scores_of, c), ck), :]
                s = jnp.dot(k, qst_ref[...], preferred_element_type=F32)
                s_ref[rows, :R] = s
                part = jnp.max(s, axis=0, keepdims=True)
                col_max = part if col_max is None else jnp.maximum(col_max, part)
        if scores_of is not None:
            smax_ref[...] = col_max

    slot(scores_of=0)
    slot(scores_of=1, softmax=True)

    def steady(u, carry):
        slot(scores_of=u, softmax=True, pv_of=u - 2)
        return carry

    lax.fori_loop(2, n_kv, steady, 0, unroll=2 if n_kv - 2 >= 4 else 1)
    slot(softmax=True, pv_of=n_kv - 2)
    slot(pv_of=n_kv - 1)


def _stack_queries(x, qst_ref, cols, k_halves):
    xt = x.astype(F32).T
    zero = jnp.zeros((HEAD_DIM, xt.shape[1]), F32)
    for part, c, k_half in zip((xt[:HEAD_DIM], xt[HEAD_DIM:]), cols, k_halves):
        block = [part, zero] if k_half == 0 else [zero, part]
        qst_ref[:, c] = jnp.concatenate(block, axis=0).astype(BF16)


def _attn_scratch(R, tkv, acc_rows):
    return [pltpu.VMEM((LANES, R), BF16), pltpu.VMEM((1, R), F32),
            pltpu.VMEM((acc_rows, R + LANES), F32), pltpu.VMEM((tkv, R + LANES), F32),
            pltpu.VMEM((1, R), F32), pltpu.VMEM((1, R), F32),
            pltpu.VMEM((tkv, R), BF16)]


def _gqa_kernel(q_ref, k_ref, vt_ref, o_ref, qst_ref, m_ref, acc_ref, *pipe, tq, tkv, n_kv):
    per_kv = GQA_HEADS // GQA_KV
    col = lambda h: slice(h * tq, (h + 1) * tq)
    for j in range(GQA_HEADS // 2):
        heads = (2 * j, 2 * j + 1)
        _stack_queries(q_ref[0, :, j * LANES:(j + 1) * LANES], qst_ref,
                       [col(h) for h in heads], [h // per_kv for h in heads])
    half = per_kv * tq
    _flash_cols(qst_ref, k_ref, vt_ref, m_ref, acc_ref, *pipe, tkv=tkv, n_kv=n_kv,
                pv_groups=((0, slice(0, half)), (HEAD_DIM + ONES_ROWS, slice(half, 2 * half))))
    out, den = slice(0, HEAD_DIM), slice(HEAD_DIM, HEAD_DIM + 1)
    for j in range(GQA_HEADS // 2):
        c0, c1 = col(2 * j), col(2 * j + 1)
        top = acc_ref[out, c0] * (1.0 / acc_ref[den, c0])
        bot = acc_ref[out, c1] * (1.0 / acc_ref[den, c1])
        o_ref[0, :, j * LANES:(j + 1) * LANES] = (
            jnp.concatenate([top, bot], axis=0).T.astype(BF16))


def _gqa(q, k, vt, t):
    B, S, W = q.shape
    tq, tkv = t["gqa_tq"], t["tkv"]
    kern = functools.partial(_gqa_kernel, tq=tq, tkv=tkv, n_kv=S // tkv)
    return pl.pallas_call(
        kern,
        grid=(B, S // tq),
        in_specs=[pl.BlockSpec((1, tq, W), lambda b, i: (b, i, 0)),
                  pl.BlockSpec((1, S, LANES), lambda b, i: (b, 0, 0)),
                  pl.BlockSpec((GQA_KV * (HEAD_DIM + ONES_ROWS), S), lambda b, i: (0, b))],
        out_specs=pl.BlockSpec((1, tq, W), lambda b, i: (b, i, 0)),
        out_shape=jax.ShapeDtypeStruct((B, S, W), BF16),
        scratch_shapes=_attn_scratch(GQA_HEADS * tq, tkv, HEAD_DIM + ONES_ROWS),
        compiler_params=_params(2),
        name="gqa",
    )(q, k, vt)


def _diff_kernel(lam_ref, on_ref, q_ref, k_ref, vt_ref, o_ref, qst_ref, m_ref, acc_ref,
                 *pipe, tq, tkv, n_kv, lambda_init):
    c0, c1 = slice(0, tq), slice(tq, 2 * tq)
    _stack_queries(q_ref[0], qst_ref, (c0, c1), (0, 1))
    _flash_cols(qst_ref, k_ref, vt_ref, m_ref, acc_ref, *pipe, tkv=tkv, n_kv=n_kv,
                pv_groups=((0, slice(0, 2 * tq)),))
    lv = lam_ref[...]
    lam = (jnp.exp(jnp.sum(lv[0:1] * lv[1:2], axis=1, keepdims=True))
           - jnp.exp(jnp.sum(lv[2:3] * lv[3:4], axis=1, keepdims=True)) + lambda_init)
    out, den = slice(0, DIFF_V), slice(DIFF_V, DIFF_V + 1)
    ot = (acc_ref[out, c0] * (1.0 / acc_ref[den, c0])
          - acc_ref[out, c1] * (lam / acc_ref[den, c1]))
    otn = ot * lax.rsqrt(jnp.mean(ot * ot, axis=0, keepdims=True) + EPS)
    o_ref[0] = (otn.T * on_ref[...] * (1.0 - lambda_init)).astype(BF16)


def _diff(q, k, vt, lam, out_norm, lambda_init, t):
    B, S, W = q.shape
    tq, tkv = t["diff_tq"], t["tkv"]
    kern = functools.partial(_diff_kernel, tq=tq, tkv=tkv, n_kv=S // tkv,
                             lambda_init=lambda_init)
    return pl.pallas_call(
        kern,
        grid=(B, DIFF_HEADS, S // tq),
        in_specs=[pl.BlockSpec((4, HEAD_DIM), lambda b, h, i: (0, 0)),
                  pl.BlockSpec((1, DIFF_V), lambda b, h, i: (0, 0)),
                  pl.BlockSpec((1, tq, LANES), lambda b, h, i: (b, i, h)),
                  pl.BlockSpec((1, S, LANES), lambda b, h, i: (b, 0, h)),
                  pl.BlockSpec((DIFF_V + ONES_ROWS, S), lambda b, h, i: (h, b))],
        out_specs=pl.BlockSpec((1, tq, LANES), lambda b, h, i: (b, i, h)),
        out_shape=jax.ShapeDtypeStruct((B, S, W), BF16),
        scratch_shapes=_attn_scratch(2 * tq, tkv, DIFF_V + ONES_ROWS),
        compiler_params=_params(3),
        name="diff",
    )(lam, out_norm, q, k, vt)


def _merge_kernel(x_ref, conv_ref, convp_ref, convn_ref, pool_ref, poolp_ref, pooln_ref,
                  og_ref, od_ref, gate_ref, cw_ref, pw_ref, band_ref, ps_ref, wb_ref, wo_ref, o_ref,
                  cbuf, pbuf, *, tm, n_s, seq):
    si = pl.program_id(0) % n_s
    keep_prev = (si > 0).astype(F32)
    keep_next = (si < n_s - 1).astype(F32)
    W = BRANCH_W

    def gated(ref):
        blk = ref[...].astype(F32)
        return blk[:, 2 * W:3 * W] * blk[:, 0:W]
    cbuf[0:HALO, :] = gated(convp_ref) * keep_prev
    cbuf[HALO:HALO + tm, :] = gated(conv_ref)
    cbuf[HALO + tm:2 * HALO + tm, :] = gated(convn_ref) * keep_next
    cw = cw_ref[...]
    conv = (cw[0:1] * cbuf[HALO - 1:HALO - 1 + tm, :] + cw[1:2] * cbuf[HALO:HALO + tm, :]
            + cw[2:3] * cbuf[HALO + 1:HALO + 1 + tm, :])
    br_a = (conv_ref[:, W:2 * W].astype(F32) * conv).astype(BF16)

    zeros = jnp.zeros((HALO, W), BF16)
    pbuf[0:HALO, :] = jnp.where(si > 0, poolp_ref[...], zeros)
    pbuf[HALO:HALO + tm, :] = pool_ref[...]
    pbuf[HALO + tm:2 * HALO + tm, :] = jnp.where(si < n_s - 1, pooln_ref[...], zeros)
    pos = si * tm + lax.broadcasted_iota(jnp.int32, (tm, 1), 0)
    pooled = []
    for gi, win in enumerate(POOL_WINDOWS):
        lanes = slice(gi * POOL_GROUP, (gi + 1) * POOL_GROUP)
        half = win // 2
        tot = jnp.concatenate(
            [jnp.dot(band_ref[gi], pbuf[r0:r0 + POOL_BLOCK + 2 * HALO, lanes],
                     preferred_element_type=F32) for r0 in range(0, tm, POOL_BLOCK)], axis=0)
        cnt = (jnp.clip(pos - half + win, 0, seq) - jnp.clip(pos - half, 0, seq)).astype(F32)
        mean_diff = (tot / cnt - pbuf[HALO:HALO + tm, lanes].astype(F32)).astype(BF16)
        pooled.append(jnp.dot(mean_diff, pw_ref[gi], preferred_element_type=F32))
    br_b = (jnp.concatenate(pooled, axis=1) * ps_ref[...]).astype(BF16)

    merged = None
    for n, br in enumerate((br_a, br_b, og_ref[...], od_ref[...])):
        y = jnp.dot(br, wb_ref[n], preferred_element_type=F32)
        y = gate_ref[:, n * D_MODEL:(n + 1) * D_MODEL].astype(F32) * y
        merged = y if merged is None else merged + y
    o_ref[...] = x_ref[...] + jnp.dot(merged.astype(BF16), wo_ref[...],
                                      preferred_element_type=F32)


def _merge(x2, conv, pool, og, od, gates, lw, seq, tm):
    T = x2.shape[0]
    n_s = seq // tm
    per = tm // HALO
    last = T // HALO - 1
    tok = lambda w: pl.BlockSpec((tm, w), lambda i: (i, 0))
    prev = lambda w: pl.BlockSpec((HALO, w), lambda i: (jnp.maximum(i * per - 1, 0), 0))
    nxt = lambda w: pl.BlockSpec((HALO, w), lambda i: (jnp.minimum((i + 1) * per, last), 0))
    kern = functools.partial(_merge_kernel, tm=tm, n_s=n_s, seq=seq)
    return pl.pallas_call(
        kern,
        grid=(T // tm,),
        in_specs=[tok(D_MODEL),
                  tok(3 * BRANCH_W), prev(3 * BRANCH_W), nxt(3 * BRANCH_W),
                  tok(BRANCH_W), prev(BRANCH_W), nxt(BRANCH_W),
                  tok(BRANCH_W), tok(BRANCH_W), tok(N_BRANCH * D_MODEL),
                  _resident((3, BRANCH_W)), _resident((4, POOL_GROUP, POOL_GROUP)),
                  _resident((len(POOL_WINDOWS), POOL_BLOCK, POOL_BLOCK + 2 * HALO)),
                  _resident((1, BRANCH_W)), _resident((N_BRANCH, BRANCH_W, D_MODEL)),
                  _resident((D_MODEL, D_MODEL))],
        out_specs=tok(D_MODEL),
        out_shape=jax.ShapeDtypeStruct((T, D_MODEL), F32),
        scratch_shapes=[pltpu.VMEM((tm + 2 * HALO, BRANCH_W), F32),
                        pltpu.VMEM((tm + 2 * HALO, BRANCH_W), BF16)],
        compiler_params=_params(1),
        name="merge",
    )(x2, conv, conv, conv, pool, pool, pool, og, od, gates,
      lw["conv_w"], lw["pool_w"], _pool_bands(), lw["pool_scale"], lw["w_branch"], lw["w_out"])


def _pool_bands():
    r = np.arange(POOL_BLOCK)[:, None] + HALO
    c = np.arange(POOL_BLOCK + 2 * HALO)[None, :]
    return jnp.asarray(np.stack([(c >= r - win // 2) & (c < r + win // 2)
                                 for win in POOL_WINDOWS]), BF16)


def _rope_tables(seq):
    d = np.arange(LANES) % HEAD_DIM
    t = jnp.arange(seq)

    def tables(pos, n, freq_idx, first_half, roped):
        freq = jnp.exp(-math.log(n[1]) * jnp.asarray(freq_idx, F32) * (2.0 / n[0]))
        ang = pos.astype(F32) * freq[None, :]
        c, s = jnp.cos(ang), jnp.sin(ang)
        roped, first_half = jnp.asarray(roped)[None, :], jnp.asarray(first_half)[None, :]
        return (jnp.where(roped, c, 1.0), jnp.where(roped & first_half, -s, 0.0),
                jnp.where(roped & ~first_half, s, 0.0))

    half_dim = HEAD_DIM // 2
    r = d % half_dim
    pos_axial = jnp.where(jnp.asarray(d < half_dim)[None, :], (t // GRID_W)[:, None],
                          (t % GRID_W)[:, None])
    axial = tables(pos_axial, (half_dim, AXIAL_THETA), r % (half_dim // 2), r < half_dim // 2,
                   np.ones(LANES, bool))
    partial = tables(jnp.broadcast_to(t[:, None], (seq, LANES)), (ROPE_DIM, ROPE_THETA),
                     d % (ROPE_DIM // 2), d < ROPE_DIM // 2, d < ROPE_DIM)
    return axial + partial


def _layer_weights(i, ffn1_norm, ffn1_w_in, ffn1_w_out, mix_norm, w_in, b_gate, conv_w, pool_w,
                   pool_scale, attn_q_norm, attn_k_norm, diff_q_norm, diff_k_norm, diff_lambda,
                   diff_out_norm, w_branch, w_out, ffn2_norm, ffn2_w_in, ffn2_w_out):
    two = lambda g: jnp.concatenate([g, g]).reshape(1, LANES).astype(F32)
    blk = np.arange(2 * LANES) // HEAD_DIM
    return dict(
        ffn1=(ffn1_norm[i].reshape(1, D_MODEL), ffn1_w_in[i].astype(BF16),
              ffn1_w_out[i].astype(BF16)),
        ffn2=(ffn2_norm[i].reshape(1, D_MODEL), ffn2_w_in[i].astype(BF16),
              ffn2_w_out[i].astype(BF16)),
        mix_norm=mix_norm[i].reshape(1, D_MODEL),
        w_in=w_in[i].astype(BF16),
        b_gate=b_gate[i].reshape(1, N_BRANCH * D_MODEL),
        bd=jnp.asarray(blk[:, None] == blk[None, :], BF16),
        q_norm=two(attn_q_norm[i]), k_norm=two(attn_k_norm[i]),
        dq_norm=two(diff_q_norm[i]), dk_norm=two(diff_k_norm[i]),
        diff_lambda=diff_lambda[i].astype(F32),
        diff_out_norm=diff_out_norm[i].reshape(1, DIFF_V),
        lambda_init=0.8 - 0.6 * math.exp(-0.3 * i),
        conv_w=conv_w[i], pool_w=pool_w[i].astype(BF16),
        pool_scale=pool_scale[i].reshape(1, BRANCH_W),
        w_branch=w_branch[i].astype(BF16), w_out=w_out[i].astype(BF16),
    )


def _trunk(x, layers):
    B, S, _ = x.shape
    T = B * S
    t = _tiles(S)
    tabs = _rope_tables(S)
    x2 = x.reshape(T, D_MODEL)
    for lw in layers:
        x2 = _ffn(x2, *lw["ffn1"], t["ffn_tm"])
        conv, pool, q, k, vt, dq, dk, dvt, gates = _proj(x2, lw, tabs, S, t["proj_tm"])
        seq3 = lambda a: a.reshape(B, S, a.shape[-1])
        og = _gqa(seq3(q), seq3(k), vt, t)
        od = _diff(seq3(dq), seq3(dk), dvt, lw["diff_lambda"], lw["diff_out_norm"],
                   lw["lambda_init"], t)
        x2 = _merge(x2, conv, pool, og.reshape(T, -1), od.reshape(T, -1), gates, lw, S,
                    t["merge_tm"])
        x2 = _ffn(x2, *lw["ffn2"], t["ffn_tm"])
    return x2.reshape(B, S, D_MODEL)


def kernel(x_prompt, x_sample, ffn1_norm, ffn1_w_in, ffn1_w_out, mix_norm, w_in, b_gate, conv_w, pool_w, pool_scale, attn_q_norm, attn_k_norm, diff_q_norm, diff_k_norm, diff_lambda, diff_out_norm, w_branch, w_out, ffn2_norm, ffn2_w_in, ffn2_w_out):
    weights = (ffn1_norm, ffn1_w_in, ffn1_w_out, mix_norm, w_in, b_gate, conv_w, pool_w,
               pool_scale, attn_q_norm, attn_k_norm, diff_q_norm, diff_k_norm, diff_lambda,
               diff_out_norm, w_branch, w_out, ffn2_norm, ffn2_w_in, ffn2_w_out)
    layers = [_layer_weights(i, *weights) for i in range(DEPTH)]
    return (_trunk(x_prompt, layers), _trunk(x_sample, layers))
```

```python
import functools
import math

import jax
import jax.numpy as jnp
import numpy as np
from jax import lax
from jax.experimental import pallas as pl
from jax.experimental.pallas import tpu as pltpu

F32 = jnp.float32
BF16 = jnp.bfloat16

D_MODEL = 1024
DEPTH = 2
GRID_W = 64
HEAD_DIM = 64
D_FF = 2816
N_BRANCH = 4
BRANCH_W = 512
POOL_WINDOWS = (2, 4, 8, 16)
POOL_GROUP = BRANCH_W // 4
GQA_HEADS = 8
GQA_KV = 2
DIFF_HEADS = 4
DIFF_V = 2 * HEAD_DIM
AXIAL_THETA = 10000.0
ROPE_THETA = 500000.0
ROPE_DIM = HEAD_DIM // 4
EPS = 1e-6

_OFF_CONV = 0
_OFF_POOL = 3 * BRANCH_W
_OFF_CQ = _OFF_POOL + BRANCH_W
_OFF_CK = _OFF_CQ + GQA_HEADS * HEAD_DIM
_OFF_CV = _OFF_CK + GQA_KV * HEAD_DIM
_OFF_DQ = _OFF_CV + GQA_KV * HEAD_DIM
_OFF_DK = _OFF_DQ + DIFF_HEADS * 2 * HEAD_DIM
_OFF_DV = _OFF_DK + DIFF_HEADS * 2 * HEAD_DIM
_OFF_GATE = _OFF_DV + DIFF_HEADS * DIFF_V
IN_W = _OFF_GATE + N_BRANCH * D_MODEL

LANES = 128
BF16_SUBLANES = 16
VMEM_LIMIT_BYTES = 56 * 1024 * 1024

HALO = BF16_SUBLANES
ONES_ROWS = BF16_SUBLANES
NEG_BIG = -1e30
POOL_BLOCK = 128
MXU_DEPTH = 256
SCORE_TILE_ELEMS = 1024 * 2048


def _tiles(seq):
    tkv = min(1024, seq // 4)
    cols = SCORE_TILE_ELEMS // tkv
    return dict(
        ffn_tm=512,
        proj_tm=256,
        merge_tm=512,
        gqa_tq=min(cols // GQA_HEADS, seq),
        diff_tq=min(cols // 2, seq),
        tkv=tkv,
    )


def _resident(shape):
    zeros = (0,) * len(shape)
    return pl.BlockSpec(shape, lambda *_: zeros, pipeline_mode=pl.Buffered(1))


def _params(n_axes):
    return pltpu.CompilerParams(dimension_semantics=("arbitrary",) * n_axes,
                                vmem_limit_bytes=VMEM_LIMIT_BYTES)


def _rms(xf, gain):
    return xf * lax.rsqrt(jnp.mean(xf * xf, axis=-1, keepdims=True) + EPS) * gain


def _ffn_kernel(x_ref, g_ref, win_ref, wout_ref, o_ref):
    x = x_ref[...]
    u = _rms(x, g_ref[...]).astype(BF16)
    h = jnp.dot(u, win_ref[...], preferred_element_type=F32)
    a = h[:, :D_FF]
    b = h[:, D_FF:]
    act = (a * jax.nn.sigmoid(a) * b).astype(BF16)
    y = jnp.dot(act, wout_ref[...], preferred_element_type=F32)
    o_ref[...] = x + 0.5 * y


def _ffn(x2, gain, w_in, w_out, tm):
    T = x2.shape[0]
    return pl.pallas_call(
        _ffn_kernel,
        grid=(T // tm,),
        in_specs=[pl.BlockSpec((tm, D_MODEL), lambda i: (i, 0)),
                  _resident((1, D_MODEL)),
                  _resident((D_MODEL, 2 * D_FF)),
                  _resident((D_FF, D_MODEL))],
        out_specs=pl.BlockSpec((tm, D_MODEL), lambda i: (i, 0)),
        out_shape=jax.ShapeDtypeStruct((T, D_MODEL), F32),
        compiler_params=_params(1),
        name="ffn",
    )(x2, gain, w_in, w_out)


def _head_sum_squares(z, bd):
    return jnp.dot((z * z).astype(BF16), bd, preferred_element_type=F32)


def _head_norm_rope(z, ss, gain, cos, sin_a, sin_b, shift, scale):
    zn = z * lax.rsqrt(ss * (1.0 / HEAD_DIM) + EPS) * gain
    out = zn * cos + pltpu.roll(zn, LANES - shift, 1) * sin_a + pltpu.roll(zn, shift, 1) * sin_b
    if scale != 1.0:
        out = out * scale
    return out.astype(BF16)


def _proj_kernel(x_ref, g_ref, w_ref, bgate_ref, bd_ref, qn_ref, kn_ref, dqn_ref, dkn_ref,
                 acos_ref, asa_ref, asb_ref, pcos_ref, psa_ref, psb_ref,
                 conv_o, pool_o, q_o, k_o, vt_o, dq_o, dk_o, dvt_o, gate_o):
    u = _rms(x_ref[...], g_ref[...]).astype(BF16)

    def proj(lo, width):
        return jnp.dot(u, w_ref[:, lo:lo + width], preferred_element_type=F32)

    conv_o[...] = proj(_OFF_CONV, 3 * BRANCH_W).astype(BF16)
    pool_o[...] = proj(_OFF_POOL, BRANCH_W).astype(BF16)

    bd = bd_ref[...]
    axial = (acos_ref[...], asa_ref[...], asb_ref[...], HEAD_DIM // 4)
    partial = (pcos_ref[...], psa_ref[...], psb_ref[...], ROPE_DIM // 2)
    q_scale = HEAD_DIM ** -0.5 * math.log2(math.e)

    def heads(z, width, out_ref, gain_ref, tables, scale):
        gain = gain_ref[...]
        for j in range(0, width, 2 * LANES):
            ss = _head_sum_squares(z[:, j:j + 2 * LANES], bd)
            for lo in range(j, min(j + 2 * LANES, width), LANES):
                sl = slice(lo, lo + LANES)
                out_ref[:, sl] = _head_norm_rope(z[:, sl], ss[:, lo - j:lo - j + LANES], gain,
                                                 *tables, scale)

    def values_t(v, n_heads, ch, out_ref):
        vt = v.T.astype(BF16)
        ones = jnp.ones((ONES_ROWS, vt.shape[1]), BF16)
        rows = ch + ONES_ROWS
        for h in range(n_heads):
            out_ref[h * rows:h * rows + ch, :] = vt[h * ch:(h + 1) * ch, :]
            out_ref[h * rows + ch:(h + 1) * rows, :] = ones

    n_q, n_k = GQA_HEADS * HEAD_DIM, GQA_KV * HEAD_DIM
    heads(proj(_OFF_CQ, n_q), n_q, q_o, qn_ref, axial, q_scale)
    kv = proj(_OFF_CK, 2 * n_k)
    heads(kv, n_k, k_o, kn_ref, axial, 1.0)
    values_t(kv[:, n_k:], GQA_KV, HEAD_DIM, vt_o)
    n_d = DIFF_HEADS * 2 * HEAD_DIM
    heads(proj(_OFF_DQ, n_d), n_d, dq_o, dqn_ref, partial, q_scale)
    heads(proj(_OFF_DK, n_d), n_d, dk_o, dkn_ref, partial, 1.0)
    values_t(proj(_OFF_DV, DIFF_HEADS * DIFF_V), DIFF_HEADS, DIFF_V, dvt_o)

    for n in range(N_BRANCH):
        sl = slice(n * D_MODEL, (n + 1) * D_MODEL)
        g = proj(_OFF_GATE + n * D_MODEL, D_MODEL) + bgate_ref[:, sl]
        gate_o[:, sl] = jax.nn.sigmoid(g).astype(BF16)


def _proj(x2, lw, tabs, seq, tm):
    T = x2.shape[0]
    n_s = seq // tm
    tok = lambda w: pl.BlockSpec((tm, w), lambda i: (i, 0))
    tab = pl.BlockSpec((tm, LANES), lambda i: (i % n_s, 0))
    tok_t = lambda w: pl.BlockSpec((w, tm), lambda i: (0, i))
    outs = ((3 * BRANCH_W, False), (BRANCH_W, False), (GQA_HEADS * HEAD_DIM, False),
            (GQA_KV * HEAD_DIM, False), (GQA_KV * (HEAD_DIM + ONES_ROWS), True),
            (DIFF_HEADS * 2 * HEAD_DIM, False), (DIFF_HEADS * 2 * HEAD_DIM, False),
            (DIFF_HEADS * (DIFF_V + ONES_ROWS), True), (N_BRANCH * D_MODEL, False))
    return pl.pallas_call(
        _proj_kernel,
        grid=(T // tm,),
        in_specs=[tok(D_MODEL), _resident((1, D_MODEL)), _resident((D_MODEL, IN_W)),
                  _resident((1, N_BRANCH * D_MODEL)), _resident((2 * LANES, 2 * LANES)),
                  _resident((1, LANES)), _resident((1, LANES)), _resident((1, LANES)),
                  _resident((1, LANES))] + [tab] * 6,
        out_specs=[tok_t(w) if tr else tok(w) for w, tr in outs],
        out_shape=[jax.ShapeDtypeStruct((w, T) if tr else (T, w), BF16) for w, tr in outs],
        compiler_params=_params(1),
        name="proj",
    )(x2, lw["mix_norm"], lw["w_in"], lw["b_gate"], lw["bd"], lw["q_norm"], lw["k_norm"],
      lw["dq_norm"], lw["dk_norm"], *tabs)


def _flash_cols(qst_ref, k_ref, vt_ref, m_ref, acc_ref, s_ref, smax_ref, alpha_ref, p_ref,
                *, tkv, n_kv, pv_groups):
    m_ref[...] = jnp.full(m_ref.shape, NEG_BIG, F32)
    acc_ref[...] = jnp.zeros(acc_ref.shape, F32)

    R = m_ref.shape[1]
    ck = min(MXU_DEPTH, tkv)
    n_chunks = tkv // ck

    def kv_start(u, c):
        base = u * tkv if isinstance(u, int) else pl.multiple_of(u * tkv, tkv)
        return base + c * ck

    def slot(scores_of=None, softmax=False, pv_of=None):
        if pv_of is not None:
            alpha = alpha_ref[...]
        if softmax:
            m_prev = m_ref[...]
            m_new = jnp.maximum(m_prev, smax_ref[...])
            alpha_ref[...] = jnp.exp2(m_prev - m_new)
            m_ref[...] = m_new
        col_max = None
        for c in range(n_chunks):
            rows = slice(c * ck, (c + 1) * ck)
            if pv_of is not None:
                kv = pl.ds(kv_start(pv_of, c), ck)
                for row0, cols in pv_groups:
                    vt = vt_ref[row0:row0 + acc_ref.shape[0], kv]
                    part = jnp.dot(vt, p_ref[rows, cols], preferred_element_type=F32)
                    if c == 0:
                        acc_ref[:, cols] = alpha[:, cols] * acc_ref[:, cols] + part
                    else:
                        acc_ref[:, cols] += part
            if softmax:
                p_ref[rows, :] = jnp.exp2(s_ref[rows, :R] - m_new).astype(BF16)
            if scores_of is not None:
                k = k_ref[0, pl.ds(kv_start(scores_of, c), ck), :]
                s = jnp.dot(k, qst_ref[...], preferred_element_type=F32)
                s_ref[rows, :R] = s
                part = jnp.max(s, axis=0, keepdims=True)
                col_max = part if col_max is None else jnp.maximum(col_max, part)
        if scores_of is not None:
            smax_ref[...] = col_max

    slot(scores_of=0)
    slot(scores_of=1, softmax=True)

    def steady(u, carry):
        slot(scores_of=u, softmax=True, pv_of=u - 2)
        return carry

    lax.fori_loop(2, n_kv, steady, 0, unroll=2 if n_kv - 2 >= 4 else 1)
    slot(softmax=True, pv_of=n_kv - 2)
    slot(pv_of=n_kv - 1)


def _stack_queries(x, qst_ref, cols, k_halves):
    xt = x.astype(F32).T
    zero = jnp.zeros((HEAD_DIM, xt.shape[1]), F32)
    for part, c, k_half in zip((xt[:HEAD_DIM], xt[HEAD_DIM:]), cols, k_halves):
        block = [part, zero] if k_half == 0 else [zero, part]
        qst_ref[:, c] = jnp.concatenate(block, axis=0).astype(BF16)


def _attn_scratch(R, tkv, acc_rows):
    return [pltpu.VMEM((LANES, R), BF16), pltpu.VMEM((1, R), F32),
            pltpu.VMEM((acc_rows, R + LANES), F32), pltpu.VMEM((tkv, R + LANES), F32),
            pltpu.VMEM((1, R), F32), pltpu.VMEM((1, R), F32),
            pltpu.VMEM((tkv, R), BF16)]


def _gqa_kernel(q_ref, k_ref, vt_ref, o_ref, qst_ref, m_ref, acc_ref, *pipe, tq, tkv, n_kv):
    per_kv = GQA_HEADS // GQA_KV
    col = lambda h: slice(h * tq, (h + 1) * tq)
    for j in range(GQA_HEADS // 2):
        heads = (2 * j, 2 * j + 1)
        _stack_queries(q_ref[0, :, j * LANES:(j + 1) * LANES], qst_ref,
                       [col(h) for h in heads], [h // per_kv for h in heads])
    half = per_kv * tq
    _flash_cols(qst_ref, k_ref, vt_ref, m_ref, acc_ref, *pipe, tkv=tkv, n_kv=n_kv,
                pv_groups=((0, slice(0, half)), (HEAD_DIM + ONES_ROWS, slice(half, 2 * half))))
    out, den = slice(0, HEAD_DIM), slice(HEAD_DIM, HEAD_DIM + 1)
    for j in range(GQA_HEADS // 2):
        c0, c1 = col(2 * j), col(2 * j + 1)
        top = acc_ref[out, c0] * (1.0 / acc_ref[den, c0])
        bot = acc_ref[out, c1] * (1.0 / acc_ref[den, c1])
        o_ref[0, :, j * LANES:(j + 1) * LANES] = (
            jnp.concatenate([top, bot], axis=0).T.astype(BF16))


def _gqa(q, k, vt, t):
    B, S, W = q.shape
    tq, tkv = t["gqa_tq"], t["tkv"]
    kern = functools.partial(_gqa_kernel, tq=tq, tkv=tkv, n_kv=S // tkv)
    return pl.pallas_call(
        kern,
        grid=(B, S // tq),
        in_specs=[pl.BlockSpec((1, tq, W), lambda b, i: (b, i, 0)),
                  pl.BlockSpec((1, S, LANES), lambda b, i: (b, 0, 0)),
                  pl.BlockSpec((GQA_KV * (HEAD_DIM + ONES_ROWS), S), lambda b, i: (0, b))],
        out_specs=pl.BlockSpec((1, tq, W), lambda b, i: (b, i, 0)),
        out_shape=jax.ShapeDtypeStruct((B, S, W), BF16),
        scratch_shapes=_attn_scratch(GQA_HEADS * tq, tkv, HEAD_DIM + ONES_ROWS),
        compiler_params=_params(2),
        name="gqa",
    )(q, k, vt)


def _diff_kernel(lam_ref, on_ref, q_ref, k_ref, vt_ref, o_ref, qst_ref, m_ref, acc_ref,
                 *pipe, tq, tkv, n_kv, lambda_init):
    c0, c1 = slice(0, tq), slice(tq, 2 * tq)
    _stack_queries(q_ref[0], qst_ref, (c0, c1), (0, 1))
    _flash_cols(qst_ref, k_ref, vt_ref, m_ref, acc_ref, *pipe, tkv=tkv, n_kv=n_kv,
                pv_groups=((0, slice(0, 2 * tq)),))
    lv = lam_ref[...]
    lam = (jnp.exp(jnp.sum(lv[0:1] * lv[1:2], axis=1, keepdims=True))
           - jnp.exp(jnp.sum(lv[2:3] * lv[3:4], axis=1, keepdims=True)) + lambda_init)
    out, den = slice(0, DIFF_V), slice(DIFF_V, DIFF_V + 1)
    ot = (acc_ref[out, c0] * (1.0 / acc_ref[den, c0])
          - acc_ref[out, c1] * (lam / acc_ref[den, c1]))
    otn = ot * lax.rsqrt(jnp.mean(ot * ot, axis=0, keepdims=True) + EPS)
    o_ref[0] = (otn.T * on_ref[...] * (1.0 - lambda_init)).astype(BF16)


def _diff(q, k, vt, lam, out_norm, lambda_init, t):
    B, S, W = q.shape
    tq, tkv = t["diff_tq"], t["tkv"]
    kern = functools.partial(_diff_kernel, tq=tq, tkv=tkv, n_kv=S // tkv,
                             lambda_init=lambda_init)
    return pl.pallas_call(
        kern,
        grid=(B, DIFF_HEADS, S // tq),
        in_specs=[pl.BlockSpec((4, HEAD_DIM), lambda b, h, i: (0, 0)),
                  pl.BlockSpec((1, DIFF_V), lambda b, h, i: (0, 0)),
                  pl.BlockSpec((1, tq, LANES), lambda b, h, i: (b, i, h)),
                  pl.BlockSpec((1, S, LANES), lambda b, h, i: (b, 0, h)),
                  pl.BlockSpec((DIFF_V + ONES_ROWS, S), lambda b, h, i: (h, b))],
        out_specs=pl.BlockSpec((1, tq, LANES), lambda b, h, i: (b, i, h)),
        out_shape=jax.ShapeDtypeStruct((B, S, W), BF16),
        scratch_shapes=_attn_scratch(2 * tq, tkv, DIFF_V + ONES_ROWS),
        compiler_params=_params(3),
        name="diff",
    )(lam, out_norm, q, k, vt)


def _merge_kernel(x_ref, conv_ref, convp_ref, convn_ref, pool_ref, poolp_ref, pooln_ref,
                  og_ref, od_ref, gate_ref, cw_ref, pw_ref, band_ref, ps_ref, wb_ref, wo_ref, o_ref,
                  cbuf, pbuf, *, tm, n_s, seq):
    si = pl.program_id(0) % n_s
    keep_prev = (si > 0).astype(F32)
    keep_next = (si < n_s - 1).astype(F32)
    W = BRANCH_W

    def gated(ref):
        blk = ref[...].astype(F32)
        return blk[:, 2 * W:3 * W] * blk[:, 0:W]
    cbuf[0:HALO, :] = gated(convp_ref) * keep_prev
    cbuf[HALO:HALO + tm, :] = gated(conv_ref)
    cbuf[HALO + tm:2 * HALO + tm, :] = gated(convn_ref) * keep_next
    cw = cw_ref[...]
    conv = (cw[0:1] * cbuf[HALO - 1:HALO - 1 + tm, :] + cw[1:2] * cbuf[HALO:HALO + tm, :]
            + cw[2:3] * cbuf[HALO + 1:HALO + 1 + tm, :])
    br_a = (conv_ref[:, W:2 * W].astype(F32) * conv).astype(BF16)

    zeros = jnp.zeros((HALO, W), BF16)
    pbuf[0:HALO, :] = jnp.where(si > 0, poolp_ref[...], zeros)
    pbuf[HALO:HALO + tm, :] = pool_ref[...]
    pbuf[HALO + tm:2 * HALO + tm, :] = jnp.where(si < n_s - 1, pooln_ref[...], zeros)
    pos = si * tm + lax.broadcasted_iota(jnp.int32, (tm, 1), 0)
    pooled = []
    for gi, win in enumerate(POOL_WINDOWS):
        lanes = slice(gi * POOL_GROUP, (gi + 1) * POOL_GROUP)
        half = win // 2
        tot = jnp.concatenate(
            [jnp.dot(band_ref[gi], pbuf[r0:r0 + POOL_BLOCK + 2 * HALO, lanes],
                     preferred_element_type=F32) for r0 in range(0, tm, POOL_BLOCK)], axis=0)
        cnt = (jnp.clip(pos - half + win, 0, seq) - jnp.clip(pos - half, 0, seq)).astype(F32)
        mean_diff = (tot / cnt - pbuf[HALO:HALO + tm, lanes].astype(F32)).astype(BF16)
        pooled.append(jnp.dot(mean_diff, pw_ref[gi], preferred_element_type=F32))
    br_b = (jnp.concatenate(pooled, axis=1) * ps_ref[...]).astype(BF16)

    merged = None
    for n, br in enumerate((br_a, br_b, og_ref[...], od_ref[...])):
        y = jnp.dot(br, wb_ref[n], preferred_element_type=F32)
        y = gate_ref[:, n * D_MODEL:(n + 1) * D_MODEL].astype(F32) * y
        merged = y if merged is None else merged + y
    o_ref[...] = x_ref[...] + jnp.dot(merged.astype(BF16), wo_ref[...],
                                      preferred_element_type=F32)


def _merge(x2, conv, pool, og, od, gates, lw, seq, tm):
    T = x2.shape[0]
    n_s = seq // tm
    per = tm // HALO
    last = T // HALO - 1
    tok = lambda w: pl.BlockSpec((tm, w), lambda i: (i, 0))
    prev = lambda w: pl.BlockSpec((HALO, w), lambda i: (jnp.maximum(i * per - 1, 0), 0))
    nxt = lambda w: pl.BlockSpec((HALO, w), lambda i: (jnp.minimum((i + 1) * per, last), 0))
    kern = functools.partial(_merge_kernel, tm=tm, n_s=n_s, seq=seq)
    return pl.pallas_call(
        kern,
        grid=(T // tm,),
        in_specs=[tok(D_MODEL),
                  tok(3 * BRANCH_W), prev(3 * BRANCH_W), nxt(3 * BRANCH_W),
                  tok(BRANCH_W), prev(BRANCH_W), nxt(BRANCH_W),
                  tok(BRANCH_W), tok(BRANCH_W), tok(N_BRANCH * D_MODEL),
                  _resident((3, BRANCH_W)), _resident((4, POOL_GROUP, POOL_GROUP)),
                  _resident((len(POOL_WINDOWS), POOL_BLOCK, POOL_BLOCK + 2 * HALO)),
                  _resident((1, BRANCH_W)), _resident((N_BRANCH, BRANCH_W, D_MODEL)),
                  _resident((D_MODEL, D_MODEL))],
        out_specs=tok(D_MODEL),
        out_shape=jax.ShapeDtypeStruct((T, D_MODEL), F32),
        scratch_shapes=[pltpu.VMEM((tm + 2 * HALO, BRANCH_W), F32),
                        pltpu.VMEM((tm + 2 * HALO, BRANCH_W), BF16)],
        compiler_params=_params(1),
        name="merge",
    )(x2, conv, conv, conv, pool, pool, pool, og, od, gates,
      lw["conv_w"], lw["pool_w"], _pool_bands(), lw["pool_scale"], lw["w_branch"], lw["w_out"])


def _pool_bands():
    r = np.arange(POOL_BLOCK)[:, None] + HALO
    c = np.arange(POOL_BLOCK + 2 * HALO)[None, :]
    return jnp.asarray(np.stack([(c >= r - win // 2) & (c < r + win // 2)
                                 for win in POOL_WINDOWS]), BF16)


def _rope_tables(seq):
    d = np.arange(LANES) % HEAD_DIM
    t = jnp.arange(seq)

    def tables(pos, n, freq_idx, first_half, roped):
        freq = jnp.exp(-math.log(n[1]) * jnp.asarray(freq_idx, F32) * (2.0 / n[0]))
        ang = pos.astype(F32) * freq[None, :]
        c, s = jnp.cos(ang), jnp.sin(ang)
        roped, first_half = jnp.asarray(roped)[None, :], jnp.asarray(first_half)[None, :]
        return (jnp.where(roped, c, 1.0), jnp.where(roped & first_half, -s, 0.0),
                jnp.where(roped & ~first_half, s, 0.0))

    half_dim = HEAD_DIM // 2
    r = d % half_dim
    pos_axial = jnp.where(jnp.asarray(d < half_dim)[None, :], (t // GRID_W)[:, None],
                          (t % GRID_W)[:, None])
    axial = tables(pos_axial, (half_dim, AXIAL_THETA), r % (half_dim // 2), r < half_dim // 2,
                   np.ones(LANES, bool))
    partial = tables(jnp.broadcast_to(t[:, None], (seq, LANES)), (ROPE_DIM, ROPE_THETA),
                     d % (ROPE_DIM // 2), d < ROPE_DIM // 2, d < ROPE_DIM)
    return axial + partial


def _layer_weights(i, ffn1_norm, ffn1_w_in, ffn1_w_out, mix_norm, w_in, b_gate, conv_w, pool_w,
                   pool_scale, attn_q_norm, attn_k_norm, diff_q_norm, diff_k_norm, diff_lambda,
                   diff_out_norm, w_branch, w_out, ffn2_norm, ffn2_w_in, ffn2_w_out):
    two = lambda g: jnp.concatenate([g, g]).reshape(1, LANES).astype(F32)
    blk = np.arange(2 * LANES) // HEAD_DIM
    return dict(
        ffn1=(ffn1_norm[i].reshape(1, D_MODEL), ffn1_w_in[i].astype(BF16),
              ffn1_w_out[i].astype(BF16)),
        ffn2=(ffn2_norm[i].reshape(1, D_MODEL), ffn2_w_in[i].astype(BF16),
              ffn2_w_out[i].astype(BF16)),
        mix_norm=mix_norm[i].reshape(1, D_MODEL),
        w_in=w_in[i].astype(BF16),
        b_gate=b_gate[i].reshape(1, N_BRANCH * D_MODEL),
        bd=jnp.asarray(blk[:, None] == blk[None, :], BF16),
        q_norm=two(attn_q_norm[i]), k_norm=two(attn_k_norm[i]),
        dq_norm=two(diff_q_norm[i]), dk_norm=two(diff_k_norm[i]),
        diff_lambda=diff_lambda[i].astype(F32),
        diff_out_norm=diff_out_norm[i].reshape(1, DIFF_V),
        lambda_init=0.8 - 0.6 * math.exp(-0.3 * i),
        conv_w=conv_w[i], pool_w=pool_w[i].astype(BF16),
        pool_scale=pool_scale[i].reshape(1, BRANCH_W),
        w_branch=w_branch[i].astype(BF16), w_out=w_out[i].astype(BF16),
    )


def _trunk(x, layers):
    B, S, D = x.shape
    T = B * S
    t = _tiles(S)
    assert D == D_MODEL and x.dtype == F32
    assert S % GRID_W == 0 and all(S % size == 0 for size in t.values()), (S, t)
    tabs = _rope_tables(S)
    x2 = x.reshape(T, D_MODEL)
    for lw in layers:
        x2 = _ffn(x2, *lw["ffn1"], t["ffn_tm"])
        conv, pool, q, k, vt, dq, dk, dvt, gates = _proj(x2, lw, tabs, S, t["proj_tm"])
        seq3 = lambda a: a.reshape(B, S, a.shape[-1])
        og = _gqa(seq3(q), seq3(k), vt, t)
        od = _diff(seq3(dq), seq3(dk), dvt, lw["diff_lambda"], lw["diff_out_norm"],
                   lw["lambda_init"], t)
        x2 = _merge(x2, conv, pool, og.reshape(T, -1), od.reshape(T, -1), gates, lw, S,
                    t["merge_tm"])
        x2 = _ffn(x2, *lw["ffn2"], t["ffn_tm"])
    return x2.reshape(B, S, D_MODEL)


def kernel(x_prompt, x_sample, ffn1_norm, ffn1_w_in, ffn1_w_out, mix_norm, w_in, b_gate, conv_w, pool_w, pool_scale, attn_q_norm, attn_k_norm, diff_q_norm, diff_k_norm, diff_lambda, diff_out_norm, w_branch, w_out, ffn2_norm, ffn2_w_in, ffn2_w_out):
    weights = (ffn1_norm, ffn1_w_in, ffn1_w_out, mix_norm, w_in, b_gate, conv_w, pool_w,
               pool_scale, attn_q_norm, attn_k_norm, diff_q_norm, diff_k_norm, diff_lambda,
               diff_out_norm, w_branch, w_out, ffn2_norm, ffn2_w_in, ffn2_w_out)
    layers = [_layer_weights(i, *weights) for i in range(DEPTH)]
    return (_trunk(x_prompt, layers), _trunk(x_sample, layers))
```

```python
import functools
import math

import jax
import jax.numpy as jnp
import numpy as np
from jax import lax
from jax.experimental import pallas as pl
from jax.experimental.pallas import tpu as pltpu

F32 = jnp.float32
BF16 = jnp.bfloat16

D_MODEL = 1024
DEPTH = 2
GRID_W = 64
HEAD_DIM = 64
D_FF = 2816
N_BRANCH = 4
BRANCH_W = 512
POOL_WINDOWS = (2, 4, 8, 16)
POOL_GROUP = BRANCH_W // 4
GQA_HEADS = 8
GQA_KV = 2
DIFF_HEADS = 4
DIFF_V = 2 * HEAD_DIM
AXIAL_THETA = 10000.0
ROPE_THETA = 500000.0
ROPE_DIM = HEAD_DIM // 4
EPS = 1e-6

_OFF_CONV = 0
_OFF_POOL = 3 * BRANCH_W
_OFF_CQ = _OFF_POOL + BRANCH_W
_OFF_CK = _OFF_CQ + GQA_HEADS * HEAD_DIM
_OFF_CV = _OFF_CK + GQA_KV * HEAD_DIM
_OFF_DQ = _OFF_CV + GQA_KV * HEAD_DIM
_OFF_DK = _OFF_DQ + DIFF_HEADS * 2 * HEAD_DIM
_OFF_DV = _OFF_DK + DIFF_HEADS * 2 * HEAD_DIM
_OFF_GATE = _OFF_DV + DIFF_HEADS * DIFF_V
IN_W = _OFF_GATE + N_BRANCH * D_MODEL

LANES = 128
BF16_SUBLANES = 16
VMEM_LIMIT_BYTES = 56 * 1024 * 1024

HALO = BF16_SUBLANES
ONES_ROWS = BF16_SUBLANES
POOL_BLOCK = 128
MXU_DEPTH = 256
SCORE_TILE_ELEMS = 1024 * 2048


def _tiles(seq):
    tkv = min(1024, seq // 4)
    cols = SCORE_TILE_ELEMS // tkv
    return dict(
        ffn_tm=512,
        proj_tm=256,
        merge_tm=512,
        gqa_tq=min(cols // GQA_HEADS, seq),
        diff_tq=min(cols // 2, seq),
        tkv=tkv,
    )


def _resident(shape):
    zeros = (0,) * len(shape)
    return pl.BlockSpec(shape, lambda *_: zeros, pipeline_mode=pl.Buffered(1))


def _params(n_axes):
    return pltpu.CompilerParams(dimension_semantics=("arbitrary",) * n_axes,
                                vmem_limit_bytes=VMEM_LIMIT_BYTES)


def _rms(xf, gain):
    return xf * lax.rsqrt(jnp.mean(xf * xf, axis=-1, keepdims=True) + EPS) * gain


def _ffn_kernel(x_ref, g_ref, win_ref, wout_ref, o_ref):
    x = x_ref[...]
    u = _rms(x, g_ref[...]).astype(BF16)
    h = jnp.dot(u, win_ref[...], preferred_element_type=F32)
    a = h[:, :D_FF]
    b = h[:, D_FF:]
    act = (a * jax.nn.sigmoid(a) * b).astype(BF16)
    y = jnp.dot(act, wout_ref[...], preferred_element_type=F32)
    o_ref[...] = x + 0.5 * y


def _ffn(x2, gain, w_in, w_out, tm):
    T = x2.shape[0]
    return pl.pallas_call(
        _ffn_kernel,
        grid=(T // tm,),
        in_specs=[pl.BlockSpec((tm, D_MODEL), lambda i: (i, 0)),
                  _resident((1, D_MODEL)),
                  _resident((D_MODEL, 2 * D_FF)),
                  _resident((D_FF, D_MODEL))],
        out_specs=pl.BlockSpec((tm, D_MODEL), lambda i: (i, 0)),
        out_shape=jax.ShapeDtypeStruct((T, D_MODEL), F32),
        compiler_params=_params(1),
        name="ffn",
    )(x2, gain, w_in, w_out)


def _head_sum_squares(z, bd):
    return jnp.dot((z * z).astype(BF16), bd, preferred_element_type=F32)


def _head_norm_rope(z, ss, gain, cos, sin_a, sin_b, shift, scale):
    zn = z * lax.rsqrt(ss * (1.0 / HEAD_DIM) + EPS) * gain
    out = zn * cos + pltpu.roll(zn, LANES - shift, 1) * sin_a + pltpu.roll(zn, shift, 1) * sin_b
    if scale != 1.0:
        out = out * scale
    return out.astype(BF16)


def _proj_kernel(x_ref, g_ref, w_ref, bgate_ref, bd_ref, qn_ref, kn_ref, dqn_ref, dkn_ref,
                 acos_ref, asa_ref, asb_ref, pcos_ref, psa_ref, psb_ref,
                 conv_o, pool_o, q_o, k_o, vt_o, dq_o, dk_o, dvt_o, gate_o):
    u = _rms(x_ref[...], g_ref[...]).astype(BF16)

    def proj(lo, width):
        return jnp.dot(u, w_ref[:, lo:lo + width], preferred_element_type=F32)

    conv_o[...] = proj(_OFF_CONV, 3 * BRANCH_W).astype(BF16)
    pool_o[...] = proj(_OFF_POOL, BRANCH_W).astype(BF16)

    bd = bd_ref[...]
    axial = (acos_ref[...], asa_ref[...], asb_ref[...], HEAD_DIM // 4)
    partial = (pcos_ref[...], psa_ref[...], psb_ref[...], ROPE_DIM // 2)
    q_scale = HEAD_DIM ** -0.5 * math.log2(math.e)

    def heads(z, width, out_ref, gain_ref, tables, scale):
        gain = gain_ref[...]
        for j in range(0, width, 2 * LANES):
            ss = _head_sum_squares(z[:, j:j + 2 * LANES], bd)
            for lo in range(j, min(j + 2 * LANES, width), LANES):
                sl = slice(lo, lo + LANES)
                out_ref[:, sl] = _head_norm_rope(z[:, sl], ss[:, lo - j:lo - j + LANES], gain,
                                                 *tables, scale)

    def values_t(v, n_heads, ch, out_ref):
        vt = v.T.astype(BF16)
        ones = jnp.ones((ONES_ROWS, vt.shape[1]), BF16)
        rows = ch + ONES_ROWS
        for h in range(n_heads):
            out_ref[h * rows:h * rows + ch, :] = vt[h * ch:(h + 1) * ch, :]
            out_ref[h * rows + ch:(h + 1) * rows, :] = ones

    n_q, n_k = GQA_HEADS * HEAD_DIM, GQA_KV * HEAD_DIM
    heads(proj(_OFF_CQ, n_q), n_q, q_o, qn_ref, axial, q_scale)
    kv = proj(_OFF_CK, 2 * n_k)
    heads(kv, n_k, k_o, kn_ref, axial, 1.0)
    values_t(kv[:, n_k:], GQA_KV, HEAD_DIM, vt_o)
    n_d = DIFF_HEADS * 2 * HEAD_DIM
    heads(proj(_OFF_DQ, n_d), n_d, dq_o, dqn_ref, partial, q_scale)
    heads(proj(_OFF_DK, n_d), n_d, dk_o, dkn_ref, partial, 1.0)
    values_t(proj(_OFF_DV, DIFF_HEADS * DIFF_V), DIFF_HEADS, DIFF_V, dvt_o)

    for n in range(N_BRANCH):
        sl = slice(n * D_MODEL, (n + 1) * D_MODEL)
        g = proj(_OFF_GATE + n * D_MODEL, D_MODEL) + bgate_ref[:, sl]
        gate_o[:, sl] = jax.nn.sigmoid(g).astype(BF16)


def _proj(x2, lw, tabs, seq, tm):
    T = x2.shape[0]
    n_s = seq // tm
    tok = lambda w: pl.BlockSpec((tm, w), lambda i: (i, 0))
    tab = pl.BlockSpec((tm, LANES), lambda i: (i % n_s, 0))
    tok_t = lambda w: pl.BlockSpec((w, tm), lambda i: (0, i))
    outs = ((3 * BRANCH_W, False), (BRANCH_W, False), (GQA_HEADS * HEAD_DIM, False),
            (GQA_KV * HEAD_DIM, False), (GQA_KV * (HEAD_DIM + ONES_ROWS), True),
            (DIFF_HEADS * 2 * HEAD_DIM, False), (DIFF_HEADS * 2 * HEAD_DIM, False),
            (DIFF_HEADS * (DIFF_V + ONES_ROWS), True), (N_BRANCH * D_MODEL, False))
    return pl.pallas_call(
        _proj_kernel,
        grid=(T // tm,),
        in_specs=[tok(D_MODEL), _resident((1, D_MODEL)), _resident((D_MODEL, IN_W)),
                  _resident((1, N_BRANCH * D_MODEL)), _resident((2 * LANES, 2 * LANES)),
                  _resident((1, LANES)), _resident((1, LANES)), _resident((1, LANES)),
                  _resident((1, LANES))] + [tab] * 6,
        out_specs=[tok_t(w) if tr else tok(w) for w, tr in outs],
        out_shape=[jax.ShapeDtypeStruct((w, T) if tr else (T, w), BF16) for w, tr in outs],
        compiler_params=_params(1),
        name="proj",
    )(x2, lw["mix_norm"], lw["w_in"], lw["b_gate"], lw["bd"], lw["q_norm"], lw["k_norm"],
      lw["dq_norm"], lw["dk_norm"], *tabs)


def _flash_cols(qst_ref, k_ref, vt_ref, m_ref, acc_ref, s_ref, smax_ref, alpha_ref, p_ref,
                *, tkv, n_kv, pv_groups):
    m_ref[...] = jnp.full(m_ref.shape, -jnp.inf, F32)
    acc_ref[...] = jnp.zeros(acc_ref.shape, F32)

    R = m_ref.shape[1]
    ck = min(MXU_DEPTH, tkv)
    n_chunks = tkv // ck

    def kv_start(u, c):
        base = u * tkv if isinstance(u, int) else pl.multiple_of(u * tkv, tkv)
        return base + c * ck

    def slot(scores_of=None, softmax=False, pv_of=None):
        if pv_of is not None:
            alpha = alpha_ref[...]
        if softmax:
            m_prev = m_ref[...]
            m_new = jnp.maximum(m_prev, smax_ref[...])
            alpha_ref[...] = jnp.exp2(m_prev - m_new)
            m_ref[...] = m_new
        col_max = None
        for c in range(n_chunks):
            rows = slice(c * ck, (c + 1) * ck)
            if pv_of is not None:
                kv = pl.ds(kv_start(pv_of, c), ck)
                for row0, cols in pv_groups:
                    vt = vt_ref[row0:row0 + acc_ref.shape[0], kv]
                    part = jnp.dot(vt, p_ref[rows, cols], preferred_element_type=F32)
                    if c == 0:
                        acc_ref[:, cols] = alpha[:, cols] * acc_ref[:, cols] + part
                    else:
                        acc_ref[:, cols] += part
            if softmax:
                p_ref[rows, :] = jnp.exp2(s_ref[rows, :R] - m_new).astype(BF16)
            if scores_of is not None:
                k = k_ref[0, pl.ds(kv_start(scores_of, c), ck), :]
                s = jnp.dot(k, qst_ref[...], preferred_element_type=F32)
                s_ref[rows, :R] = s
                part = jnp.max(s, axis=0, keepdims=True)
                col_max = part if col_max is None else jnp.maximum(col_max, part)
        if scores_of is not None:
            smax_ref[...] = col_max

    slot(scores_of=0)
    slot(scores_of=1, softmax=True)

    def steady(u, carry):
        slot(scores_of=u, softmax=True, pv_of=u - 2)
        return carry

    lax.fori_loop(2, n_kv, steady, 0, unroll=2 if n_kv - 2 >= 4 else 1)
    slot(softmax=True, pv_of=n_kv - 2)
    slot(pv_of=n_kv - 1)


def _stack_queries(x, qst_ref, cols, k_halves):
    xt = x.astype(F32).T
    zero = jnp.zeros((HEAD_DIM, xt.shape[1]), F32)
    for part, c, k_half in zip((xt[:HEAD_DIM], xt[HEAD_DIM:]), cols, k_halves):
        block = [part, zero] if k_half == 0 else [zero, part]
        qst_ref[:, c] = jnp.concatenate(block, axis=0).astype(BF16)


def _attn_scratch(R, tkv, acc_rows):
    return [pltpu.VMEM((LANES, R), BF16), pltpu.VMEM((1, R), F32),
            pltpu.VMEM((acc_rows, R + LANES), F32), pltpu.VMEM((tkv, R + LANES), F32),
            pltpu.VMEM((1, R), F32), pltpu.VMEM((1, R), F32),
            pltpu.VMEM((tkv, R), BF16)]


def _gqa_kernel(q_ref, k_ref, vt_ref, o_ref, qst_ref, m_ref, acc_ref, *pipe, tq, tkv, n_kv):
    per_kv = GQA_HEADS // GQA_KV
    col = lambda h: slice(h * tq, (h + 1) * tq)
    for j in range(GQA_HEADS // 2):
        heads = (2 * j, 2 * j + 1)
        _stack_queries(q_ref[0, :, j * LANES:(j + 1) * LANES], qst_ref,
                       [col(h) for h in heads], [h // per_kv for h in heads])
    half = per_kv * tq
    _flash_cols(qst_ref, k_ref, vt_ref, m_ref, acc_ref, *pipe, tkv=tkv, n_kv=n_kv,
                pv_groups=((0, slice(0, half)), (HEAD_DIM + ONES_ROWS, slice(half, 2 * half))))
    out, den = slice(0, HEAD_DIM), slice(HEAD_DIM, HEAD_DIM + 1)
    for j in range(GQA_HEADS // 2):
        c0, c1 = col(2 * j), col(2 * j + 1)
        top = acc_ref[out, c0] * (1.0 / acc_ref[den, c0])
        bot = acc_ref[out, c1] * (1.0 / acc_ref[den, c1])
        o_ref[0, :, j * LANES:(j + 1) * LANES] = (
            jnp.concatenate([top, bot], axis=0).T.astype(BF16))


def _gqa(q, k, vt, t):
    B, S, W = q.shape
    tq, tkv = t["gqa_tq"], t["tkv"]
    kern = functools.partial(_gqa_kernel, tq=tq, tkv=tkv, n_kv=S // tkv)
    return pl.pallas_call(
        kern,
        grid=(B, S // tq),
        in_specs=[pl.BlockSpec((1, tq, W), lambda b, i: (b, i, 0)),
                  pl.BlockSpec((1, S, LANES), lambda b, i: (b, 0, 0)),
                  pl.BlockSpec((GQA_KV * (HEAD_DIM + ONES_ROWS), S), lambda b, i: (0, b))],
        out_specs=pl.BlockSpec((1, tq, W), lambda b, i: (b, i, 0)),
        out_shape=jax.ShapeDtypeStruct((B, S, W), BF16),
        scratch_shapes=_attn_scratch(GQA_HEADS * tq, tkv, HEAD_DIM + ONES_ROWS),
        compiler_params=_params(2),
        name="gqa",
    )(q, k, vt)


def _diff_kernel(lam_ref, on_ref, q_ref, k_ref, vt_ref, o_ref, qst_ref, m_ref, acc_ref,
                 *pipe, tq, tkv, n_kv, lambda_init):
    c0, c1 = slice(0, tq), slice(tq, 2 * tq)
    _stack_queries(q_ref[0], qst_ref, (c0, c1), (0, 1))
    _flash_cols(qst_ref, k_ref, vt_ref, m_ref, acc_ref, *pipe, tkv=tkv, n_kv=n_kv,
                pv_groups=((0, slice(0, 2 * tq)),))
    lv = lam_ref[...]
    lam = (jnp.exp(jnp.sum(lv[0:1] * lv[1:2], axis=1, keepdims=True))
           - jnp.exp(jnp.sum(lv[2:3] * lv[3:4], axis=1, keepdims=True)) + lambda_init)
    out, den = slice(0, DIFF_V), slice(DIFF_V, DIFF_V + 1)
    ot = (acc_ref[out, c0] * (1.0 / acc_ref[den, c0])
          - acc_ref[out, c1] * (lam / acc_ref[den, c1]))
    otn = ot * lax.rsqrt(jnp.mean(ot * ot, axis=0, keepdims=True) + EPS)
    o_ref[0] = (otn.T * on_ref[...] * (1.0 - lambda_init)).astype(BF16)


def _diff(q, k, vt, lam, out_norm, lambda_init, t):
    B, S, W = q.shape
    tq, tkv = t["diff_tq"], t["tkv"]
    kern = functools.partial(_diff_kernel, tq=tq, tkv=tkv, n_kv=S // tkv,
                             lambda_init=lambda_init)
    return pl.pallas_call(
        kern,
        grid=(B, DIFF_HEADS, S // tq),
        in_specs=[pl.BlockSpec((4, HEAD_DIM), lambda b, h, i: (0, 0)),
                  pl.BlockSpec((1, DIFF_V), lambda b, h, i: (0, 0)),
                  pl.BlockSpec((1, tq, LANES), lambda b, h, i: (b, i, h)),
                  pl.BlockSpec((1, S, LANES), lambda b, h, i: (b, 0, h)),
                  pl.BlockSpec((DIFF_V + ONES_ROWS, S), lambda b, h, i: (h, b))],
        out_specs=pl.BlockSpec((1, tq, LANES), lambda b, h, i: (b, i, h)),
        out_shape=jax.ShapeDtypeStruct((B, S, W), BF16),
        scratch_shapes=_attn_scratch(2 * tq, tkv, DIFF_V + ONES_ROWS),
        compiler_params=_params(3),
        name="diff",
    )(lam, out_norm, q, k, vt)


def _merge_kernel(x_ref, conv_ref, convp_ref, convn_ref, pool_ref, poolp_ref, pooln_ref,
                  og_ref, od_ref, gate_ref, cw_ref, pw_ref, band_ref, ps_ref, wb_ref, wo_ref, o_ref,
                  cbuf, pbuf, *, tm, n_s, seq):
    si = pl.program_id(0) % n_s
    keep_prev = (si > 0).astype(F32)
    keep_next = (si < n_s - 1).astype(F32)
    W = BRANCH_W

    def gated(ref):
        blk = ref[...].astype(F32)
        return blk[:, 2 * W:3 * W] * blk[:, 0:W]
    cbuf[0:HALO, :] = gated(convp_ref) * keep_prev
    cbuf[HALO:HALO + tm, :] = gated(conv_ref)
    cbuf[HALO + tm:2 * HALO + tm, :] = gated(convn_ref) * keep_next
    cw = cw_ref[...]
    conv = (cw[0:1] * cbuf[HALO - 1:HALO - 1 + tm, :] + cw[1:2] * cbuf[HALO:HALO + tm, :]
            + cw[2:3] * cbuf[HALO + 1:HALO + 1 + tm, :])
    br_a = (conv_ref[:, W:2 * W].astype(F32) * conv).astype(BF16)

    zeros = jnp.zeros((HALO, W), BF16)
    pbuf[0:HALO, :] = jnp.where(si > 0, poolp_ref[...], zeros)
    pbuf[HALO:HALO + tm, :] = pool_ref[...]
    pbuf[HALO + tm:2 * HALO + tm, :] = jnp.where(si < n_s - 1, pooln_ref[...], zeros)
    pos = si * tm + lax.broadcasted_iota(jnp.int32, (tm, 1), 0)
    pooled = []
    for gi, win in enumerate(POOL_WINDOWS):
        lanes = slice(gi * POOL_GROUP, (gi + 1) * POOL_GROUP)
        half = win // 2
        tot = jnp.concatenate(
            [jnp.dot(band_ref[gi], pbuf[r0:r0 + POOL_BLOCK + 2 * HALO, lanes],
                     preferred_element_type=F32) for r0 in range(0, tm, POOL_BLOCK)], axis=0)
        cnt = (jnp.clip(pos - half + win, 0, seq) - jnp.clip(pos - half, 0, seq)).astype(F32)
        mean_diff = (tot / cnt - pbuf[HALO:HALO + tm, lanes].astype(F32)).astype(BF16)
        pooled.append(jnp.dot(mean_diff, pw_ref[gi], preferred_element_type=F32))
    br_b = (jnp.concatenate(pooled, axis=1) * ps_ref[...]).astype(BF16)

    merged = None
    for n, br in enumerate((br_a, br_b, og_ref[...], od_ref[...])):
        y = jnp.dot(br, wb_ref[n], preferred_element_type=F32)
        y = gate_ref[:, n * D_MODEL:(n + 1) * D_MODEL].astype(F32) * y
        merged = y if merged is None else merged + y
    o_ref[...] = x_ref[...] + jnp.dot(merged.astype(BF16), wo_ref[...],
                                      preferred_element_type=F32)


def _merge(x2, conv, pool, og, od, gates, lw, seq, tm):
    T = x2.shape[0]
    n_s = seq // tm
    per = tm // HALO
    last = T // HALO - 1
    tok = lambda w: pl.BlockSpec((tm, w), lambda i: (i, 0))
    prev = lambda w: pl.BlockSpec((HALO, w), lambda i: (jnp.maximum(i * per - 1, 0), 0))
    nxt = lambda w: pl.BlockSpec((HALO, w), lambda i: (jnp.minimum((i + 1) * per, last), 0))
    kern = functools.partial(_merge_kernel, tm=tm, n_s=n_s, seq=seq)
    return pl.pallas_call(
        kern,
        grid=(T // tm,),
        in_specs=[tok(D_MODEL),
                  tok(3 * BRANCH_W), prev(3 * BRANCH_W), nxt(3 * BRANCH_W),
                  tok(BRANCH_W), prev(BRANCH_W), nxt(BRANCH_W),
                  tok(BRANCH_W), tok(BRANCH_W), tok(N_BRANCH * D_MODEL),
                  _resident((3, BRANCH_W)), _resident((4, POOL_GROUP, POOL_GROUP)),
                  _resident((len(POOL_WINDOWS), POOL_BLOCK, POOL_BLOCK + 2 * HALO)),
                  _resident((1, BRANCH_W)), _resident((N_BRANCH, BRANCH_W, D_MODEL)),
                  _resident((D_MODEL, D_MODEL))],
        out_specs=tok(D_MODEL),
        out_shape=jax.ShapeDtypeStruct((T, D_MODEL), F32),
        scratch_shapes=[pltpu.VMEM((tm + 2 * HALO, BRANCH_W), F32),
                        pltpu.VMEM((tm + 2 * HALO, BRANCH_W), BF16)],
        compiler_params=_params(1),
        name="merge",
    )(x2, conv, conv, conv, pool, pool, pool, og, od, gates,
      lw["conv_w"], lw["pool_w"], _pool_bands(), lw["pool_scale"], lw["w_branch"], lw["w_out"])


def _pool_bands():
    r = np.arange(POOL_BLOCK)[:, None] + HALO
    c = np.arange(POOL_BLOCK + 2 * HALO)[None, :]
    return jnp.asarray(np.stack([(c >= r - win // 2) & (c < r + win // 2)
                                 for win in POOL_WINDOWS]), BF16)


def _rope_tables(seq):
    d = np.arange(LANES) % HEAD_DIM
    t = jnp.arange(seq)

    def tables(pos, n, freq_idx, first_half, roped):
        freq = jnp.exp(-math.log(n[1]) * jnp.asarray(freq_idx, F32) * (2.0 / n[0]))
        ang = pos.astype(F32) * freq[None, :]
        c, s = jnp.cos(ang), jnp.sin(ang)
        roped, first_half = jnp.asarray(roped)[None, :], jnp.asarray(first_half)[None, :]
        return (jnp.where(roped, c, 1.0), jnp.where(roped & first_half, -s, 0.0),
                jnp.where(roped & ~first_half, s, 0.0))

    half_dim = HEAD_DIM // 2
    r = d % half_dim
    pos_axial = jnp.where(jnp.asarray(d < half_dim)[None, :], (t // GRID_W)[:, None],
                          (t % GRID_W)[:, None])
    axial = tables(pos_axial, (half_dim, AXIAL_THETA), r % (half_dim // 2), r < half_dim // 2,
                   np.ones(LANES, bool))
    partial = tables(jnp.broadcast_to(t[:, None], (seq, LANES)), (ROPE_DIM, ROPE_THETA),
                     d % (ROPE_DIM // 2), d < ROPE_DIM // 2, d < ROPE_DIM)
    return axial + partial


def _layer_weights(i, ffn1_norm, ffn1_w_in, ffn1_w_out, mix_norm, w_in, b_gate, conv_w, pool_w,
                   pool_scale, attn_q_norm, attn_k_norm, diff_q_norm, diff_k_norm, diff_lambda,
                   diff_out_norm, w_branch, w_out, ffn2_norm, ffn2_w_in, ffn2_w_out):
    two = lambda g: jnp.concatenate([g, g]).reshape(1, LANES).astype(F32)
    blk = np.arange(2 * LANES) // HEAD_DIM
    return dict(
        ffn1=(ffn1_norm[i].reshape(1, D_MODEL), ffn1_w_in[i].astype(BF16),
              ffn1_w_out[i].astype(BF16)),
        ffn2=(ffn2_norm[i].reshape(1, D_MODEL), ffn2_w_in[i].astype(BF16),
              ffn2_w_out[i].astype(BF16)),
        mix_norm=mix_norm[i].reshape(1, D_MODEL),
        w_in=w_in[i].astype(BF16),
        b_gate=b_gate[i].reshape(1, N_BRANCH * D_MODEL),
        bd=jnp.asarray(blk[:, None] == blk[None, :], BF16),
        q_norm=two(attn_q_norm[i]), k_norm=two(attn_k_norm[i]),
        dq_norm=two(diff_q_norm[i]), dk_norm=two(diff_k_norm[i]),
        diff_lambda=diff_lambda[i].astype(F32),
        diff_out_norm=diff_out_norm[i].reshape(1, DIFF_V),
        lambda_init=0.8 - 0.6 * math.exp(-0.3 * i),
        conv_w=conv_w[i], pool_w=pool_w[i].astype(BF16),
        pool_scale=pool_scale[i].reshape(1, BRANCH_W),
        w_branch=w_branch[i].astype(BF16), w_out=w_out[i].astype(BF16),
    )


def _trunk(x, layers):
    B, S, D = x.shape
    T = B * S
    t = _tiles(S)
    assert D == D_MODEL and x.dtype == F32
    assert S % GRID_W == 0 and all(S % size == 0 for size in t.values()), (S, t)
    tabs = _rope_tables(S)
    x2 = x.reshape(T, D_MODEL)
    for lw in layers:
        x2 = _ffn(x2, *lw["ffn1"], t["ffn_tm"])
        conv, pool, q, k, vt, dq, dk, dvt, gates = _proj(x2, lw, tabs, S, t["proj_tm"])
        seq3 = lambda a: a.reshape(B, S, a.shape[-1])
        og = _gqa(seq3(q), seq3(k), vt, t)
        od = _diff(seq3(dq), seq3(dk), dvt, lw["diff_lambda"], lw["diff_out_norm"],
                   lw["lambda_init"], t)
        x2 = _merge(x2, conv, pool, og.reshape(T, -1), od.reshape(T, -1), gates, lw, S,
                    t["merge_tm"])
        x2 = _ffn(x2, *lw["ffn2"], t["ffn_tm"])
    return x2.reshape(B, S, D_MODEL)


def kernel(x_prompt, x_sample, ffn1_norm, ffn1_w_in, ffn1_w_out, mix_norm, w_in, b_gate, conv_w, pool_w, pool_scale, attn_q_norm, attn_k_norm, diff_q_norm, diff_k_norm, diff_lambda, diff_out_norm, w_branch, w_out, ffn2_norm, ffn2_w_in, ffn2_w_out):
    weights = (ffn1_norm, ffn1_w_in, ffn1_w_out, mix_norm, w_in, b_gate, conv_w, pool_w,
               pool_scale, attn_q_norm, attn_k_norm, diff_q_norm, diff_k_norm, diff_lambda,
               diff_out_norm, w_branch, w_out, ffn2_norm, ffn2_w_in, ffn2_w_out)
    layers = [_layer_weights(i, *weights) for i in range(DEPTH)]
    return (_trunk(x_prompt, layers), _trunk(x_sample, layers))
```

```python
import functools
import math

import jax
import jax.numpy as jnp
import numpy as np
from jax import lax
from jax.experimental import pallas as pl
from jax.experimental.pallas import tpu as pltpu

F32 = jnp.float32
BF16 = jnp.bfloat16

D_MODEL = 1024
DEPTH = 2
GRID_W = 64
HEAD_DIM = 64
D_FF = 2816
N_BRANCH = 4
BRANCH_W = 512
POOL_WINDOWS = (2, 4, 8, 16)
POOL_GROUP = BRANCH_W // 4
GQA_HEADS = 8
GQA_KV = 2
DIFF_HEADS = 4
DIFF_V = 2 * HEAD_DIM
AXIAL_THETA = 10000.0
ROPE_THETA = 500000.0
ROPE_DIM = HEAD_DIM // 4
EPS = 1e-6

_OFF_CONV = 0
_OFF_POOL = 3 * BRANCH_W
_OFF_CQ = _OFF_POOL + BRANCH_W
_OFF_CK = _OFF_CQ + GQA_HEADS * HEAD_DIM
_OFF_CV = _OFF_CK + GQA_KV * HEAD_DIM
_OFF_DQ = _OFF_CV + GQA_KV * HEAD_DIM
_OFF_DK = _OFF_DQ + DIFF_HEADS * 2 * HEAD_DIM
_OFF_DV = _OFF_DK + DIFF_HEADS * 2 * HEAD_DIM
_OFF_GATE = _OFF_DV + DIFF_HEADS * DIFF_V
IN_W = _OFF_GATE + N_BRANCH * D_MODEL

LANES = 128
BF16_SUBLANES = 16
VMEM_LIMIT_BYTES = 56 * 1024 * 1024

HALO = BF16_SUBLANES
ONES_ROWS = BF16_SUBLANES
POOL_BLOCK = 128
MXU_DEPTH = 256
SCORE_TILE_ELEMS = 1024 * 2048


def _tiles(seq):
    tkv = min(1024, seq // 4)
    cols = SCORE_TILE_ELEMS // tkv
    return dict(
        ffn_tm=512,
        proj_tm=512,
        merge_tm=512,
        gqa_tq=min(cols // GQA_HEADS, seq),
        diff_tq=min(cols // 2, seq),
        tkv=tkv,
    )


def _resident(shape):
    zeros = (0,) * len(shape)
    return pl.BlockSpec(shape, lambda *_: zeros, pipeline_mode=pl.Buffered(1))


def _params(n_axes):
    return pltpu.CompilerParams(dimension_semantics=("arbitrary",) * n_axes,
                                vmem_limit_bytes=VMEM_LIMIT_BYTES)


def _rms(xf, gain):
    return xf * lax.rsqrt(jnp.mean(xf * xf, axis=-1, keepdims=True) + EPS) * gain


def _ffn_kernel(x_ref, g_ref, win_ref, wout_ref, o_ref):
    x = x_ref[...]
    u = _rms(x, g_ref[...]).astype(BF16)
    h = jnp.dot(u, win_ref[...], preferred_element_type=F32)
    a = h[:, :D_FF]
    b = h[:, D_FF:]
    act = (a * jax.nn.sigmoid(a) * b).astype(BF16)
    y = jnp.dot(act, wout_ref[...], preferred_element_type=F32)
    o_ref[...] = x + 0.5 * y


def _ffn(x2, gain, w_in, w_out, tm):
    T = x2.shape[0]
    return pl.pallas_call(
        _ffn_kernel,
        grid=(T // tm,),
        in_specs=[pl.BlockSpec((tm, D_MODEL), lambda i: (i, 0)),
                  _resident((1, D_MODEL)),
                  _resident((D_MODEL, 2 * D_FF)),
                  _resident((D_FF, D_MODEL))],
        out_specs=pl.BlockSpec((tm, D_MODEL), lambda i: (i, 0)),
        out_shape=jax.ShapeDtypeStruct((T, D_MODEL), F32),
        compiler_params=_params(1),
        name="ffn",
    )(x2, gain, w_in, w_out)


def _head_sum_squares(z, bd):
    return jnp.dot((z * z).astype(BF16), bd, preferred_element_type=F32)


def _head_norm_rope(z, ss, gain, cos, sin_a, sin_b, shift, scale):
    zn = z * lax.rsqrt(ss * (1.0 / HEAD_DIM) + EPS) * gain
    out = zn * cos + pltpu.roll(zn, LANES - shift, 1) * sin_a + pltpu.roll(zn, shift, 1) * sin_b
    if scale != 1.0:
        out = out * scale
    return out.astype(BF16)


def _proj_kernel(x_ref, g_ref, w_ref, bgate_ref, bd_ref, qn_ref, kn_ref, dqn_ref, dkn_ref,
                 acos_ref, asa_ref, asb_ref, pcos_ref, psa_ref, psb_ref,
                 conv_o, pool_o, q_o, k_o, vt_o, dq_o, dk_o, dvt_o, gate_o):
    u = _rms(x_ref[...], g_ref[...]).astype(BF16)

    def proj(lo, width):
        return jnp.dot(u, w_ref[:, lo:lo + width], preferred_element_type=F32)

    conv_o[...] = proj(_OFF_CONV, 3 * BRANCH_W).astype(BF16)
    pool_o[...] = proj(_OFF_POOL, BRANCH_W).astype(BF16)

    bd = bd_ref[...]
    axial = (acos_ref[...], asa_ref[...], asb_ref[...], HEAD_DIM // 4)
    partial = (pcos_ref[...], psa_ref[...], psb_ref[...], ROPE_DIM // 2)
    q_scale = HEAD_DIM ** -0.5 * math.log2(math.e)

    def heads(z, width, out_ref, gain_ref, tables, scale):
        gain = gain_ref[...]
        for j in range(0, width, 2 * LANES):
            ss = _head_sum_squares(z[:, j:j + 2 * LANES], bd)
            for lo in range(j, min(j + 2 * LANES, width), LANES):
                sl = slice(lo, lo + LANES)
                out_ref[:, sl] = _head_norm_rope(z[:, sl], ss[:, lo - j:lo - j + LANES], gain,
                                                 *tables, scale)

    def values_t(v, n_heads, ch, out_ref):
        vt = v.T.astype(BF16)
        ones = jnp.ones((ONES_ROWS, vt.shape[1]), BF16)
        rows = ch + ONES_ROWS
        for h in range(n_heads):
            out_ref[h * rows:h * rows + ch, :] = vt[h * ch:(h + 1) * ch, :]
            out_ref[h * rows + ch:(h + 1) * rows, :] = ones

    n_q, n_k = GQA_HEADS * HEAD_DIM, GQA_KV * HEAD_DIM
    heads(proj(_OFF_CQ, n_q), n_q, q_o, qn_ref, axial, q_scale)
    kv = proj(_OFF_CK, 2 * n_k)
    heads(kv, n_k, k_o, kn_ref, axial, 1.0)
    values_t(kv[:, n_k:], GQA_KV, HEAD_DIM, vt_o)
    n_d = DIFF_HEADS * 2 * HEAD_DIM
    heads(proj(_OFF_DQ, n_d), n_d, dq_o, dqn_ref, partial, q_scale)
    heads(proj(_OFF_DK, n_d), n_d, dk_o, dkn_ref, partial, 1.0)
    values_t(proj(_OFF_DV, DIFF_HEADS * DIFF_V), DIFF_HEADS, DIFF_V, dvt_o)

    for n in range(N_BRANCH):
        sl = slice(n * D_MODEL, (n + 1) * D_MODEL)
        g = proj(_OFF_GATE + n * D_MODEL, D_MODEL) + bgate_ref[:, sl]
        gate_o[:, sl] = jax.nn.sigmoid(g).astype(BF16)


def _proj(x2, lw, tabs, seq, tm):
    T = x2.shape[0]
    n_s = seq // tm
    tok = lambda w: pl.BlockSpec((tm, w), lambda i: (i, 0))
    tab = pl.BlockSpec((tm, LANES), lambda i: (i % n_s, 0))
    tok_t = lambda w: pl.BlockSpec((w, tm), lambda i: (0, i))
    outs = ((3 * BRANCH_W, False), (BRANCH_W, False), (GQA_HEADS * HEAD_DIM, False),
            (GQA_KV * HEAD_DIM, False), (GQA_KV * (HEAD_DIM + ONES_ROWS), True),
            (DIFF_HEADS * 2 * HEAD_DIM, False), (DIFF_HEADS * 2 * HEAD_DIM, False),
            (DIFF_HEADS * (DIFF_V + ONES_ROWS), True), (N_BRANCH * D_MODEL, False))
    return pl.pallas_call(
        _proj_kernel,
        grid=(T // tm,),
        in_specs=[tok(D_MODEL), _resident((1, D_MODEL)), _resident((D_MODEL, IN_W)),
                  _resident((1, N_BRANCH * D_MODEL)), _resident((2 * LANES, 2 * LANES)),
                  _resident((1, LANES)), _resident((1, LANES)), _resident((1, LANES)),
                  _resident((1, LANES))] + [tab] * 6,
        out_specs=[tok_t(w) if tr else tok(w) for w, tr in outs],
        out_shape=[jax.ShapeDtypeStruct((w, T) if tr else (T, w), BF16) for w, tr in outs],
        compiler_params=_params(1),
        name="proj",
    )(x2, lw["mix_norm"], lw["w_in"], lw["b_gate"], lw["bd"], lw["q_norm"], lw["k_norm"],
      lw["dq_norm"], lw["dk_norm"], *tabs)


def _flash_cols(qst_ref, k_ref, vt_ref, m_ref, acc_ref, s_ref, smax_ref, alpha_ref, p_ref,
                *, tkv, n_kv, pv_groups):
    m_ref[...] = jnp.full(m_ref.shape, -jnp.inf, F32)
    acc_ref[...] = jnp.zeros(acc_ref.shape, F32)

    R = m_ref.shape[1]
    ck = min(MXU_DEPTH, tkv)
    n_chunks = tkv // ck

    def kv_start(u, c):
        base = u * tkv if isinstance(u, int) else pl.multiple_of(u * tkv, tkv)
        return base + c * ck

    def slot(scores_of=None, softmax=False, pv_of=None):
        if pv_of is not None:
            alpha = alpha_ref[...]
        if softmax:
            m_prev = m_ref[...]
            m_new = jnp.maximum(m_prev, smax_ref[...])
            alpha_ref[...] = jnp.exp2(m_prev - m_new)
            m_ref[...] = m_new
        col_max = None
        for c in range(n_chunks):
            rows = slice(c * ck, (c + 1) * ck)
            if pv_of is not None:
                kv = pl.ds(kv_start(pv_of, c), ck)
                for row0, cols in pv_groups:
                    vt = vt_ref[row0:row0 + acc_ref.shape[0], kv]
                    part = jnp.dot(vt, p_ref[rows, cols], preferred_element_type=F32)
                    if c == 0:
                        acc_ref[:, cols] = alpha[:, cols] * acc_ref[:, cols] + part
                    else:
                        acc_ref[:, cols] += part
            if softmax:
                p_ref[rows, :] = jnp.exp2(s_ref[rows, :R] - m_new).astype(BF16)
            if scores_of is not None:
                k = k_ref[0, pl.ds(kv_start(scores_of, c), ck), :]
                s = jnp.dot(k, qst_ref[...], preferred_element_type=F32)
                s_ref[rows, :R] = s
                part = jnp.max(s, axis=0, keepdims=True)
                col_max = part if col_max is None else jnp.maximum(col_max, part)
        if scores_of is not None:
            smax_ref[...] = col_max

    slot(scores_of=0)
    slot(scores_of=1, softmax=True)

    def steady(u, carry):
        slot(scores_of=u, softmax=True, pv_of=u - 2)
        return carry

    lax.fori_loop(2, n_kv, steady, 0, unroll=2 if n_kv - 2 >= 4 else 1)
    slot(softmax=True, pv_of=n_kv - 2)
    slot(pv_of=n_kv - 1)


def _stack_queries(x, qst_ref, cols, k_halves):
    xt = x.astype(F32).T
    zero = jnp.zeros((HEAD_DIM, xt.shape[1]), F32)
    for part, c, k_half in zip((xt[:HEAD_DIM], xt[HEAD_DIM:]), cols, k_halves):
        block = [part, zero] if k_half == 0 else [zero, part]
        qst_ref[:, c] = jnp.concatenate(block, axis=0).astype(BF16)


def _attn_scratch(R, tkv, acc_rows):
    return [pltpu.VMEM((LANES, R), BF16), pltpu.VMEM((1, R), F32),
            pltpu.VMEM((acc_rows, R + LANES), F32), pltpu.VMEM((tkv, R + LANES), F32),
            pltpu.VMEM((1, R), F32), pltpu.VMEM((1, R), F32),
            pltpu.VMEM((tkv, R), BF16)]


def _gqa_kernel(q_ref, k_ref, vt_ref, o_ref, qst_ref, m_ref, acc_ref, *pipe, tq, tkv, n_kv):
    per_kv = GQA_HEADS // GQA_KV
    col = lambda h: slice(h * tq, (h + 1) * tq)
    for j in range(GQA_HEADS // 2):
        heads = (2 * j, 2 * j + 1)
        _stack_queries(q_ref[0, :, j * LANES:(j + 1) * LANES], qst_ref,
                       [col(h) for h in heads], [h // per_kv for h in heads])
    half = per_kv * tq
    _flash_cols(qst_ref, k_ref, vt_ref, m_ref, acc_ref, *pipe, tkv=tkv, n_kv=n_kv,
                pv_groups=((0, slice(0, half)), (HEAD_DIM + ONES_ROWS, slice(half, 2 * half))))
    out, den = slice(0, HEAD_DIM), slice(HEAD_DIM, HEAD_DIM + 1)
    for j in range(GQA_HEADS // 2):
        c0, c1 = col(2 * j), col(2 * j + 1)
        top = acc_ref[out, c0] * (1.0 / acc_ref[den, c0])
        bot = acc_ref[out, c1] * (1.0 / acc_ref[den, c1])
        o_ref[0, :, j * LANES:(j + 1) * LANES] = (
            jnp.concatenate([top, bot], axis=0).T.astype(BF16))


def _gqa(q, k, vt, t):
    B, S, W = q.shape
    tq, tkv = t["gqa_tq"], t["tkv"]
    kern = functools.partial(_gqa_kernel, tq=tq, tkv=tkv, n_kv=S // tkv)
    return pl.pallas_call(
        kern,
        grid=(B, S // tq),
        in_specs=[pl.BlockSpec((1, tq, W), lambda b, i: (b, i, 0)),
                  pl.BlockSpec((1, S, LANES), lambda b, i: (b, 0, 0)),
                  pl.BlockSpec((GQA_KV * (HEAD_DIM + ONES_ROWS), S), lambda b, i: (0, b))],
        out_specs=pl.BlockSpec((1, tq, W), lambda b, i: (b, i, 0)),
        out_shape=jax.ShapeDtypeStruct((B, S, W), BF16),
        scratch_shapes=_attn_scratch(GQA_HEADS * tq, tkv, HEAD_DIM + ONES_ROWS),
        compiler_params=_params(2),
        name="gqa",
    )(q, k, vt)


def _diff_kernel(lam_ref, on_ref, q_ref, k_ref, vt_ref, o_ref, qst_ref, m_ref, acc_ref,
                 *pipe, tq, tkv, n_kv, lambda_init):
    c0, c1 = slice(0, tq), slice(tq, 2 * tq)
    _stack_queries(q_ref[0], qst_ref, (c0, c1), (0, 1))
    _flash_cols(qst_ref, k_ref, vt_ref, m_ref, acc_ref, *pipe, tkv=tkv, n_kv=n_kv,
                pv_groups=((0, slice(0, 2 * tq)),))
    lv = lam_ref[...]
    lam = (jnp.exp(jnp.sum(lv[0:1] * lv[1:2], axis=1, keepdims=True))
           - jnp.exp(jnp.sum(lv[2:3] * lv[3:4], axis=1, keepdims=True)) + lambda_init)
    out, den = slice(0, DIFF_V), slice(DIFF_V, DIFF_V + 1)
    ot = (acc_ref[out, c0] * (1.0 / acc_ref[den, c0])
          - acc_ref[out, c1] * (lam / acc_ref[den, c1]))
    otn = ot * lax.rsqrt(jnp.mean(ot * ot, axis=0, keepdims=True) + EPS)
    o_ref[0] = (otn.T * on_ref[...] * (1.0 - lambda_init)).astype(BF16)


def _diff(q, k, vt, lam, out_norm, lambda_init, t):
    B, S, W = q.shape
    tq, tkv = t["diff_tq"], t["tkv"]
    kern = functools.partial(_diff_kernel, tq=tq, tkv=tkv, n_kv=S // tkv,
                             lambda_init=lambda_init)
    return pl.pallas_call(
        kern,
        grid=(B, DIFF_HEADS, S // tq),
        in_specs=[pl.BlockSpec((4, HEAD_DIM), lambda b, h, i: (0, 0)),
                  pl.BlockSpec((1, DIFF_V), lambda b, h, i: (0, 0)),
                  pl.BlockSpec((1, tq, LANES), lambda b, h, i: (b, i, h)),
                  pl.BlockSpec((1, S, LANES), lambda b, h, i: (b, 0, h)),
                  pl.BlockSpec((DIFF_V + ONES_ROWS, S), lambda b, h, i: (h, b))],
        out_specs=pl.BlockSpec((1, tq, LANES), lambda b, h, i: (b, i, h)),
        out_shape=jax.ShapeDtypeStruct((B, S, W), BF16),
        scratch_shapes=_attn_scratch(2 * tq, tkv, DIFF_V + ONES_ROWS),
        compiler_params=_params(3),
        name="diff",
    )(lam, out_norm, q, k, vt)


def _merge_kernel(x_ref, conv_ref, convp_ref, convn_ref, pool_ref, poolp_ref, pooln_ref,
                  og_ref, od_ref, gate_ref, cw_ref, pw_ref, band_ref, ps_ref, wb_ref, wo_ref, o_ref,
                  cbuf, pbuf, *, tm, n_s, seq):
    si = pl.program_id(0) % n_s
    keep_prev = (si > 0).astype(F32)
    keep_next = (si < n_s - 1).astype(F32)
    W = BRANCH_W

    def gated(ref):
        blk = ref[...].astype(F32)
        return blk[:, 2 * W:3 * W] * blk[:, 0:W]
    cbuf[0:HALO, :] = gated(convp_ref) * keep_prev
    cbuf[HALO:HALO + tm, :] = gated(conv_ref)
    cbuf[HALO + tm:2 * HALO + tm, :] = gated(convn_ref) * keep_next
    cw = cw_ref[...]
    conv = (cw[0:1] * cbuf[HALO - 1:HALO - 1 + tm, :] + cw[1:2] * cbuf[HALO:HALO + tm, :]
            + cw[2:3] * cbuf[HALO + 1:HALO + 1 + tm, :])
    br_a = (conv_ref[:, W:2 * W].astype(F32) * conv).astype(BF16)

    zeros = jnp.zeros((HALO, W), BF16)
    pbuf[0:HALO, :] = jnp.where(si > 0, poolp_ref[...], zeros)
    pbuf[HALO:HALO + tm, :] = pool_ref[...]
    pbuf[HALO + tm:2 * HALO + tm, :] = jnp.where(si < n_s - 1, pooln_ref[...], zeros)
    pos = si * tm + lax.broadcasted_iota(jnp.int32, (tm, 1), 0)
    pooled = []
    for gi, win in enumerate(POOL_WINDOWS):
        lanes = slice(gi * POOL_GROUP, (gi + 1) * POOL_GROUP)
        half = win // 2
        tot = jnp.concatenate(
            [jnp.dot(band_ref[gi], pbuf[r0:r0 + POOL_BLOCK + 2 * HALO, lanes],
                     preferred_element_type=F32) for r0 in range(0, tm, POOL_BLOCK)], axis=0)
        cnt = (jnp.clip(pos - half + win, 0, seq) - jnp.clip(pos - half, 0, seq)).astype(F32)
        mean_diff = (tot / cnt - pbuf[HALO:HALO + tm, lanes].astype(F32)).astype(BF16)
        pooled.append(jnp.dot(mean_diff, pw_ref[gi], preferred_element_type=F32))
    br_b = (jnp.concatenate(pooled, axis=1) * ps_ref[...]).astype(BF16)

    merged = None
    for n, br in enumerate((br_a, br_b, og_ref[...], od_ref[...])):
        y = jnp.dot(br, wb_ref[n], preferred_element_type=F32)
        y = gate_ref[:, n * D_MODEL:(n + 1) * D_MODEL].astype(F32) * y
        merged = y if merged is None else merged + y
    o_ref[...] = x_ref[...] + jnp.dot(merged.astype(BF16), wo_ref[...],
                                      preferred_element_type=F32)


def _merge(x2, conv, pool, og, od, gates, lw, seq, tm):
    T = x2.shape[0]
    n_s = seq // tm
    per = tm // HALO
    last = T // HALO - 1
    tok = lambda w: pl.BlockSpec((tm, w), lambda i: (i, 0))
    prev = lambda w: pl.BlockSpec((HALO, w), lambda i: (jnp.maximum(i * per - 1, 0), 0))
    nxt = lambda w: pl.BlockSpec((HALO, w), lambda i: (jnp.minimum((i + 1) * per, last), 0))
    kern = functools.partial(_merge_kernel, tm=tm, n_s=n_s, seq=seq)
    return pl.pallas_call(
        kern,
        grid=(T // tm,),
        in_specs=[tok(D_MODEL),
                  tok(3 * BRANCH_W), prev(3 * BRANCH_W), nxt(3 * BRANCH_W),
                  tok(BRANCH_W), prev(BRANCH_W), nxt(BRANCH_W),
                  tok(BRANCH_W), tok(BRANCH_W), tok(N_BRANCH * D_MODEL),
                  _resident((3, BRANCH_W)), _resident((4, POOL_GROUP, POOL_GROUP)),
                  _resident((len(POOL_WINDOWS), POOL_BLOCK, POOL_BLOCK + 2 * HALO)),
                  _resident((1, BRANCH_W)), _resident((N_BRANCH, BRANCH_W, D_MODEL)),
                  _resident((D_MODEL, D_MODEL))],
        out_specs=tok(D_MODEL),
        out_shape=jax.ShapeDtypeStruct((T, D_MODEL), F32),
        scratch_shapes=[pltpu.VMEM((tm + 2 * HALO, BRANCH_W), F32),
                        pltpu.VMEM((tm + 2 * HALO, BRANCH_W), BF16)],
        compiler_params=_params(1),
        name="merge",
    )(x2, conv, conv, conv, pool, pool, pool, og, od, gates,
      lw["conv_w"], lw["pool_w"], _pool_bands(), lw["pool_scale"], lw["w_branch"], lw["w_out"])


def _pool_bands():
    r = np.arange(POOL_BLOCK)[:, None] + HALO
    c = np.arange(POOL_BLOCK + 2 * HALO)[None, :]
    return jnp.asarray(np.stack([(c >= r - win // 2) & (c < r + win // 2)
                                 for win in POOL_WINDOWS]), BF16)


def _rope_tables(seq):
    d = np.arange(LANES) % HEAD_DIM
    t = jnp.arange(seq)

    def tables(pos, n, freq_idx, first_half, roped):
        freq = jnp.exp(-math.log(n[1]) * jnp.asarray(freq_idx, F32) * (2.0 / n[0]))
        ang = pos.astype(F32) * freq[None, :]
        c, s = jnp.cos(ang), jnp.sin(ang)
        roped, first_half = jnp.asarray(roped)[None, :], jnp.asarray(first_half)[None, :]
        return (jnp.where(roped, c, 1.0), jnp.where(roped & first_half, -s, 0.0),
                jnp.where(roped & ~first_half, s, 0.0))

    half_dim = HEAD_DIM // 2
    r = d % half_dim
    pos_axial = jnp.where(jnp.asarray(d < half_dim)[None, :], (t // GRID_W)[:, None],
                          (t % GRID_W)[:, None])
    axial = tables(pos_axial, (half_dim, AXIAL_THETA), r % (half_dim // 2), r < half_dim // 2,
                   np.ones(LANES, bool))
    partial = tables(jnp.broadcast_to(t[:, None], (seq, LANES)), (ROPE_DIM, ROPE_THETA),
                     d % (ROPE_DIM // 2), d < ROPE_DIM // 2, d < ROPE_DIM)
    return axial + partial


def _layer_weights(i, ffn1_norm, ffn1_w_in, ffn1_w_out, mix_norm, w_in, b_gate, conv_w, pool_w,
                   pool_scale, attn_q_norm, attn_k_norm, diff_q_norm, diff_k_norm, diff_lambda,
                   diff_out_norm, w_branch, w_out, ffn2_norm, ffn2_w_in, ffn2_w_out):
    two = lambda g: jnp.concatenate([g, g]).reshape(1, LANES).astype(F32)
    blk = np.arange(2 * LANES) // HEAD_DIM
    return dict(
        ffn1=(ffn1_norm[i].reshape(1, D_MODEL), ffn1_w_in[i].astype(BF16),
              ffn1_w_out[i].astype(BF16)),
        ffn2=(ffn2_norm[i].reshape(1, D_MODEL), ffn2_w_in[i].astype(BF16),
              ffn2_w_out[i].astype(BF16)),
        mix_norm=mix_norm[i].reshape(1, D_MODEL),
        w_in=w_in[i].astype(BF16),
        b_gate=b_gate[i].reshape(1, N_BRANCH * D_MODEL),
        bd=jnp.asarray(blk[:, None] == blk[None, :], BF16),
        q_norm=two(attn_q_norm[i]), k_norm=two(attn_k_norm[i]),
        dq_norm=two(diff_q_norm[i]), dk_norm=two(diff_k_norm[i]),
        diff_lambda=diff_lambda[i].astype(F32),
        diff_out_norm=diff_out_norm[i].reshape(1, DIFF_V),
        lambda_init=0.8 - 0.6 * math.exp(-0.3 * i),
        conv_w=conv_w[i], pool_w=pool_w[i].astype(BF16),
        pool_scale=pool_scale[i].reshape(1, BRANCH_W),
        w_branch=w_branch[i].astype(BF16), w_out=w_out[i].astype(BF16),
    )


def _trunk(x, layers):
    B, S, D = x.shape
    T = B * S
    t = _tiles(S)
    assert D == D_MODEL and x.dtype == F32
    assert S % GRID_W == 0 and all(S % size == 0 for size in t.values()), (S, t)
    tabs = _rope_tables(S)
    x2 = x.reshape(T, D_MODEL)
    for lw in layers:
        x2 = _ffn(x2, *lw["ffn1"], t["ffn_tm"])
        conv, pool, q, k, vt, dq, dk, dvt, gates = _proj(x2, lw, tabs, S, t["proj_tm"])
        seq3 = lambda a: a.reshape(B, S, a.shape[-1])
        og = _gqa(seq3(q), seq3(k), vt, t)
        od = _diff(seq3(dq), seq3(dk), dvt, lw["diff_lambda"], lw["diff_out_norm"],
                   lw["lambda_init"], t)
        x2 = _merge(x2, conv, pool, og.reshape(T, -1), od.reshape(T, -1), gates, lw, S,
                    t["merge_tm"])
        x2 = _ffn(x2, *lw["ffn2"], t["ffn_tm"])
    return x2.reshape(B, S, D_MODEL)


def kernel(x_prompt, x_sample, ffn1_norm, ffn1_w_in, ffn1_w_out, mix_norm, w_in, b_gate, conv_w, pool_w, pool_scale, attn_q_norm, attn_k_norm, diff_q_norm, diff_k_norm, diff_lambda, diff_out_norm, w_branch, w_out, ffn2_norm, ffn2_w_in, ffn2_w_out):
    weights = (ffn1_norm, ffn1_w_in, ffn1_w_out, mix_norm, w_in, b_gate, conv_w, pool_w,
               pool_scale, attn_q_norm, attn_k_norm, diff_q_norm, diff_k_norm, diff_lambda,
               diff_out_norm, w_branch, w_out, ffn2_norm, ffn2_w_in, ffn2_w_out)
    layers = [_layer_weights(i, *weights) for i in range(DEPTH)]
    return (_trunk(x_prompt, layers), _trunk(x_sample, layers))
```
